```python
import math
import jax, jax.numpy as jnp
from jax import lax
import numpy as np

D_MODEL = 2048
BATCH = 1
SEQ = 16384
DEPTH = 2

N_META = 16
M_HEADS = 4
M_QK = 128
M_V = 256
M_CHUNK = 64
CONV_W = 4
F_HEADS = 8
F_HD = 128
Q_BLOCK = 128
D_FF = 4 * D_MODEL
M_QK_W = M_HEADS * M_QK
M_V_W = M_HEADS * M_V
F_W = F_HEADS * F_HD
MIX_WIDTH = M_V_W + F_W
SPLIT_SIZES = (M_QK_W, M_QK_W, M_V_W, M_HEADS, M_HEADS, M_V_W, F_W, F_W, F_W, F_HEADS)
IN_WIDTH = 2 * M_QK_W + 2 * M_V_W + 2 * M_HEADS + 3 * F_W + F_HEADS
DN_ALPHA = (2 * DEPTH) ** 0.25
DN_BETA = (8 * DEPTH) ** -0.25
LN_EPS = 1e-5
RMS_EPS = 1e-6
NEG = -1e30

kernel_name = "hymba_mlstm_fox_deepnorm"


def layer_norm(x, g, b):
    xf = x.astype(jnp.float32)
    mu = jnp.mean(xf, axis=-1, keepdims=True)
    var = jnp.mean(jnp.square(xf - mu), axis=-1, keepdims=True)
    return ((xf - mu) * lax.rsqrt(var + LN_EPS) * g + b).astype(x.dtype)


def head_rmsnorm(h, g):
    H, d = h.shape[-2], h.shape[-1]
    hf = h.astype(jnp.float32)
    hf = hf * lax.rsqrt(jnp.mean(jnp.square(hf), axis=-1, keepdims=True) + RMS_EPS)
    return (hf * g.reshape(H, d)).astype(h.dtype)


def causal_conv(x, w, b):
    L = x.shape[1]
    xp = jnp.pad(x, ((0, 0), (CONV_W - 1, 0), (0, 0)))
    out = b
    for k in range(CONV_W):
        out = out + xp[:, k:k + L, :] * w[k]
    return out


def pad_front(x, p, value=0.0):
    pads = [(0, 0)] * x.ndim
    pads[1] = (p, 0)
    return jnp.pad(x, pads, constant_values=value)


def mlstm(q, k, v, i_pre, f_pre):
    B, L, H, dqk = q.shape
    dv = v.shape[-1]
    dtype = v.dtype
    P = M_CHUNK - N_META
    Lp = P + L
    NC = Lp // M_CHUNK
    f32 = jnp.float32
    qp = pad_front(q.astype(f32) * (dqk ** -0.5), P)
    kp = pad_front(k.astype(f32), P)
    vp = pad_front(v.astype(f32), P)
    log_i = pad_front(i_pre.astype(f32), P, NEG)
    log_f = pad_front(jax.nn.log_sigmoid(f_pre.astype(f32)), P, 0.0)

    def chunks(a):
        return a.reshape(B, NC, M_CHUNK, H, a.shape[-1]).transpose(0, 3, 1, 2, 4)

    def gchunks(a):
        return a.reshape(B, NC, M_CHUNK, H).transpose(0, 3, 1, 2)

    qc, kc, vc = chunks(qp), chunks(kp), chunks(vp)
    li, lf = gchunks(log_i), gchunks(log_f)

    b = jnp.cumsum(lf, axis=-1)
    g = b[..., -1]
    a = g[..., None] - b + li
    m_loc = jnp.max(a, axis=-1)
    w_loc = jnp.exp(a - m_loc[..., None])
    C_loc = jnp.einsum('bhncd,bhnce,bhnc->bhnde', kc, vc, w_loc)
    n_loc = jnp.einsum('bhncd,bhnc->bhnd', kc, w_loc)

    def step(carry, inp):
        C, n, m = carry
        Cl, nl, ml, gl = inp
        m_new = jnp.maximum(gl + m, ml)
        s_prev = jnp.exp(gl + m - m_new)
        s_loc = jnp.exp(ml - m_new)
        C_new = s_prev[..., None, None] * C + s_loc[..., None, None] * Cl
        n_new = s_prev[..., None] * n + s_loc[..., None] * nl
        return (C_new, n_new, m_new), (C, n, m)

    init = (jnp.zeros((B, H, dqk, dv), f32), jnp.zeros((B, H, dqk), f32), jnp.zeros((B, H), f32))
    xs = (jnp.moveaxis(C_loc, 2, 0), jnp.moveaxis(n_loc, 2, 0),
          jnp.moveaxis(m_loc, 2, 0), jnp.moveaxis(g, 2, 0))
    _, (C_prev, n_prev, m_prev) = lax.scan(step, init, xs)
    C_prev = jnp.moveaxis(C_prev, 0, 2)
    n_prev = jnp.moveaxis(n_prev, 0, 2)
    m_prev = jnp.moveaxis(m_prev, 0, 2)

    causal = jnp.tril(jnp.ones((M_CHUNK, M_CHUNK), dtype=bool))
    D = b[..., :, None] - b[..., None, :] + li[..., None, :]
    D = jnp.where(causal, D, NEG)
    inter_log = b + m_prev[..., None]
    m_t = jnp.maximum(inter_log, jnp.max(D, axis=-1))
    W = jnp.exp(D - m_t[..., None]) * jnp.einsum('bhntd,bhnsd->bhnts', qc, kc)
    s_inter = jnp.exp(inter_log - m_t)
    num = jnp.einsum('bhnts,bhnse->bhnte', W, vc) + \
        s_inter[..., None] * jnp.einsum('bhntd,bhnde->bhnte', qc, C_prev)
    nq = jnp.sum(W, axis=-1) + s_inter * jnp.einsum('bhntd,bhnd->bhnt', qc, n_prev)
    h = num / jnp.maximum(jnp.abs(nq), jnp.exp(-m_t))[..., None]
    h = h.transpose(0, 2, 3, 1, 4).reshape(B, Lp, H, dv)[:, P:]
    return h.astype(dtype)


def forgetting_attention(q, k, v, f_pre):
    B, L, H, d = q.shape
    dtype = v.dtype
    P = Q_BLOCK - N_META
    Lp = P + L
    NB = Lp // Q_BLOCK
    f32 = jnp.float32
    qp = pad_front(q.astype(f32) * (d ** -0.5), P).transpose(0, 2, 1, 3)
    kp = pad_front(k.astype(f32), P).transpose(0, 2, 1, 3)
    vp = pad_front(v.astype(f32), P).transpose(0, 2, 1, 3)
    log_f = pad_front(jax.nn.log_sigmoid(f_pre.astype(f32)), P, 0.0)
    c = jnp.cumsum(log_f, axis=1).transpose(0, 2, 1)
    key_pos = jnp.arange(Lp)
    qb = qp.reshape(B, H, NB, Q_BLOCK, d).transpose(2, 0, 1, 3, 4)
    cb = c.reshape(B, H, NB, Q_BLOCK).transpose(2, 0, 1, 3)

    def block(args):
        idx, q_blk, c_blk = args
        q_pos = idx * Q_BLOCK + jnp.arange(Q_BLOCK)
        s = jnp.einsum('bhtd,bhsd->bhts', q_blk, kp) + c_blk[..., :, None] - c[..., None, :]
        mask = (key_pos[None, :] <= q_pos[:, None]) & (key_pos[None, :] >= P)
        p = jax.nn.softmax(jnp.where(mask, s, NEG), axis=-1)
        return jnp.einsum('bhts,bhsd->bhtd', p, vp)

    o = lax.map(block, (jnp.arange(NB, dtype=jnp.int32), qb, cb))
    o = o.transpose(1, 0, 3, 2, 4).reshape(B, Lp, H, d)[:, P:]
    return o.astype(dtype)


def hybrid_mixer(x, w_in, conv_w, conv_b, m_i_bias, m_f_bias, m_norm, f_f_bias, f_norm, w_out):
    B, L, _ = x.shape
    proj = x @ w_in
    offsets = [int(o) for o in np.cumsum(SPLIT_SIZES)[:-1]]
    mq, mk, mv, mi, mf, mo, fq, fk, fv, ff = jnp.split(proj, offsets, axis=-1)
    qk = jax.nn.silu(causal_conv(jnp.concatenate([mq, mk], axis=-1), conv_w, conv_b))
    mq, mk = jnp.split(qk, 2, axis=-1)
    h_m = mlstm(mq.reshape(B, L, M_HEADS, M_QK), mk.reshape(B, L, M_HEADS, M_QK),
                mv.reshape(B, L, M_HEADS, M_V), mi + m_i_bias, mf + m_f_bias)
    h_m = head_rmsnorm(h_m, m_norm).reshape(B, L, M_V_W) * jax.nn.sigmoid(mo)
    h_f = forgetting_attention(fq.reshape(B, L, F_HEADS, F_HD), fk.reshape(B, L, F_HEADS, F_HD),
                               fv.reshape(B, L, F_HEADS, F_HD), ff + f_f_bias)
    h_f = head_rmsnorm(h_f, f_norm).reshape(B, L, F_W)
    return jnp.concatenate([h_m, h_f], axis=-1) @ w_out


def sq_relu_mlp(x, w_up, w_down):
    return jnp.square(jax.nn.relu(x @ w_up)) @ w_down


def setup_inputs(seed: int = 0) -> dict:
    key = jax.random.key(seed)
    ks = jax.random.split(key, 17)
    f32 = jnp.float32
    nrm = lambda k, shape, s: jax.random.normal(k, shape, f32) * s
    m_f_base = jnp.linspace(3.0, 6.0, M_HEADS, dtype=f32)
    f_f_base = jnp.linspace(1.0, 5.0, F_HEADS, dtype=f32)
    return {
        "x": nrm(ks[0], (BATCH, SEQ, D_MODEL), 1.0),
        "meta": nrm(ks[1], (N_META, D_MODEL), 1.0),
        "w_in": nrm(ks[2], (DEPTH, D_MODEL, IN_WIDTH), D_MODEL ** -0.5),
        "conv_w": nrm(ks[3], (DEPTH, CONV_W, 2 * M_QK_W), CONV_W ** -0.5),
        "conv_b": nrm(ks[4], (DEPTH, 2 * M_QK_W), 0.01),
        "m_i_bias": -1.0 + nrm(ks[5], (DEPTH, M_HEADS), 0.1),
        "m_f_bias": m_f_base + nrm(ks[6], (DEPTH, M_HEADS), 0.1),
        "m_norm": 1.0 + nrm(ks[7], (DEPTH, M_V_W), 0.02),
        "f_f_bias": f_f_base + nrm(ks[8], (DEPTH, F_HEADS), 0.1),
        "f_norm": 1.0 + nrm(ks[9], (DEPTH, F_W), 0.02),
        "w_out": nrm(ks[10], (DEPTH, MIX_WIDTH, D_MODEL), DN_BETA * MIX_WIDTH ** -0.5),
        "ln1_g": 1.0 + nrm(ks[11], (DEPTH, D_MODEL), 0.02),
        "ln1_b": nrm(ks[12], (DEPTH, D_MODEL), 0.02),
        "w_up": nrm(ks[13], (DEPTH, D_MODEL, D_FF), D_MODEL ** -0.5),
        "w_down": nrm(ks[14], (DEPTH, D_FF, D_MODEL), DN_BETA * D_FF ** -0.5),
        "ln2_g": 1.0 + nrm(ks[15], (DEPTH, D_MODEL), 0.02),
        "ln2_b": nrm(ks[16], (DEPTH, D_MODEL), 0.02),
    }


def reference(x, meta, w_in, conv_w, conv_b, m_i_bias, m_f_bias, m_norm, f_f_bias, f_norm,
              w_out, ln1_g, ln1_b, w_up, w_down, ln2_g, ln2_b):
    B = x.shape[0]
    meta_b = jnp.broadcast_to(meta[None].astype(x.dtype), (B, N_META, D_MODEL))
    h = jnp.concatenate([meta_b, x], axis=1)
    for l in range(DEPTH):
        y = hybrid_mixer(h, w_in[l], conv_w[l], conv_b[l], m_i_bias[l], m_f_bias[l], m_norm[l],
                         f_f_bias[l], f_norm[l], w_out[l])
        h = layer_norm(DN_ALPHA * h + y, ln1_g[l], ln1_b[l])
        h = layer_norm(DN_ALPHA * h + sq_relu_mlp(h, w_up[l], w_down[l]), ln2_g[l], ln2_b[l])
    return h[:, N_META:]
```

```python
import functools

import jax
import jax.numpy as jnp
from jax import lax
from jax.experimental import pallas as pl
from jax.experimental.pallas import tpu as pltpu

D_MODEL = 2048
SEQ = 16384
DEPTH = 2
N_META = 16
M_HEADS = 4
M_QK = 128
M_V = 256
CONV_W = 4
F_HEADS = 8
F_HD = 128
D_FF = 4 * D_MODEL
M_QK_W = M_HEADS * M_QK
M_V_W = M_HEADS * M_V
F_W = F_HEADS * F_HD
DN_ALPHA = (2 * DEPTH) ** 0.25
LN_EPS = 1e-5
RMS_EPS = 1e-6
NEG = -1e30

LANES = 128
BLK = 256
P_PAD = BLK - N_META
LP = P_PAD + N_META + SEQ
NBLK = LP // BLK
GROUP_W = 1024
N_GROUPS = 6
TM_PROJ = 1040
TM_OUT = 520
TF = 512
C_AUG = M_V + LANES
VMEM_LIMIT = 56 * 1024 * 1024

f32 = jnp.float32
bf16 = jnp.bfloat16


def _layer_norm(z, g, b):
    mu = jnp.mean(z, axis=-1, keepdims=True)
    zc = z - mu
    var = jnp.mean(zc * zc, axis=-1, keepdims=True)
    return zc * lax.rsqrt(var + LN_EPS) * g + b


def _log_sigmoid(x):
    return jnp.minimum(x, 0.0) - jnp.log(1.0 + jnp.exp(-jnp.abs(x)))


def _gates_kernel(h_ref, w_ref, b_ref, g_ref, gt_ref, carry_ref):
    i = pl.program_id(0)

    @pl.when(i == 0)
    def _():
        carry_ref[...] = jnp.zeros_like(carry_ref)

    pre = jnp.dot(h_ref[...], w_ref[...], preferred_element_type=f32,
                  precision=lax.Precision.HIGHEST) + b_ref[...]
    row = i * BLK + lax.broadcasted_iota(jnp.int32, (BLK, LANES), 0)
    col = lax.broadcasted_iota(jnp.int32, (BLK, LANES), 1)
    valid = row >= P_PAD
    is_f = (col >= M_HEADS) & (col < 2 * M_HEADS + F_HEADS)
    lf = jnp.where(valid & is_f, _log_sigmoid(pre), 0.0)
    r2 = lax.broadcasted_iota(jnp.int32, (BLK, BLK), 0)
    c2 = lax.broadcasted_iota(jnp.int32, (BLK, BLK), 1)
    tri = jnp.where(c2 <= r2, 1.0, 0.0).astype(f32)
    cum = jnp.dot(tri, lf, preferred_element_type=f32,
                  precision=lax.Precision.HIGHEST) + carry_ref[...]
    carry_ref[...] = cum[BLK - 1:BLK, :]
    out = jnp.where(col < M_HEADS, jnp.where(valid, pre, NEG), cum)
    g_ref[...] = out
    gt_ref[...] = out.T


def _gates(h, wg, bg):
    return pl.pallas_call(
        _gates_kernel,
        grid=(NBLK,),
        in_specs=[pl.BlockSpec((BLK, D_MODEL), lambda i: (i, 0)),
                  pl.BlockSpec((D_MODEL, LANES), lambda i: (0, 0)),
                  pl.BlockSpec((1, LANES), lambda i: (0, 0))],
        out_specs=[pl.BlockSpec((BLK, LANES), lambda i: (i, 0)),
                   pl.BlockSpec((LANES, BLK), lambda i: (0, i))],
        out_shape=[jax.ShapeDtypeStruct((LP, LANES), f32),
                   jax.ShapeDtypeStruct((LANES, LP), f32)],
        scratch_shapes=[pltpu.VMEM((1, LANES), f32)],
        compiler_params=pltpu.CompilerParams(dimension_semantics=("arbitrary",)),
        name="gates",
    )(h, wg, bg)


def _inproj_kernel(x_ref, w_ref, o_ref):
    o_ref[...] = jnp.dot(x_ref[...], w_ref[...], preferred_element_type=f32).astype(o_ref.dtype)


def _inproj(xb, w):
    return pl.pallas_call(
        _inproj_kernel,
        grid=(N_GROUPS, LP // TM_PROJ),
        in_specs=[pl.BlockSpec((TM_PROJ, D_MODEL), lambda j, i: (i, 0)),
                  pl.BlockSpec((D_MODEL, GROUP_W), lambda j, i: (0, j))],
        out_specs=pl.BlockSpec((None, TM_PROJ, GROUP_W), lambda j, i: (j, i, 0)),
        out_shape=jax.ShapeDtypeStruct((N_GROUPS, LP, GROUP_W), bf16),
        compiler_params=pltpu.CompilerParams(
            dimension_semantics=("arbitrary", "arbitrary"), vmem_limit_bytes=VMEM_LIMIT),
        name="inproj",
    )(xb, w)


def _mlstm_kernel(qk_ref, v_ref, og_ref, g_ref, gt_ref, cw_ref, cb_ref, nrm_ref, out_ref,
                  c_scr, m_scr, cf_scr, conv_scr):
    i = pl.program_id(0)
    T = BLK

    @pl.when(i == 0)
    def _():
        c_scr[...] = jnp.zeros_like(c_scr)
        m_scr[...] = jnp.zeros_like(m_scr)
        cf_scr[...] = jnp.zeros_like(cf_scr)
        conv_scr[...] = jnp.zeros_like(conv_scr)

    row = i * T + lax.broadcasted_iota(jnp.int32, (T, 1), 0)
    x = jnp.where(row >= P_PAD, qk_ref[...].astype(f32), 0.0)
    xs = jnp.concatenate([conv_scr[...], x], axis=0)
    cw = cw_ref[...]
    conv = cb_ref[...]
    for k in range(CONV_W):
        off = 8 - (CONV_W - 1) + k
        conv = conv + cw[k:k + 1, :] * xs[off:off + T, :]
    conv_scr[...] = x[T - 8:T, :]
    act = conv * jax.nn.sigmoid(conv)

    G = g_ref[...]
    GT = gt_ref[...]
    cfs_all = cf_scr[...]
    cf_scr[...] = G[T - 1:T, :]
    r2 = lax.broadcasted_iota(jnp.int32, (T, T), 0)
    c2 = lax.broadcasted_iota(jnp.int32, (T, T), 1)
    causal = c2 <= r2
    ones_blk = jnp.where(lax.broadcasted_iota(jnp.int32, (T, LANES), 1) == 0, 1.0, 0.0).astype(bf16)

    for h in range(M_HEADS):
        qh = (act[:, h * M_QK:(h + 1) * M_QK] * (M_QK ** -0.5)).astype(bf16)
        kf = act[:, M_QK_W + h * M_QK:M_QK_W + (h + 1) * M_QK]
        kh = kf.astype(bf16)
        khT = kf.T.astype(bf16)
        v_aug = jnp.concatenate([v_ref[:, h * M_V:(h + 1) * M_V], ones_blk], axis=1)
        cfs = cfs_all[:, M_HEADS + h:M_HEADS + h + 1]
        b_col = G[:, M_HEADS + h:M_HEADS + h + 1] - cfs
        b_row = GT[M_HEADS + h:M_HEADS + h + 1, :] - cfs
        li_col = G[:, h:h + 1]
        li_row = GT[h:h + 1, :]
        g_tot = b_col[T - 1:T, :]
        m_prev = m_scr[h][:, 0:1]
        c_prev = c_scr[h]

        d = jnp.where(causal, b_col - b_row + li_row, NEG)
        inter_log = b_col + m_prev
        m_t = jnp.maximum(inter_log, jnp.max(d, axis=-1, keepdims=True))
        s = lax.dot_general(qh, kh, (((1,), (1,)), ((), ())), preferred_element_type=f32)
        w = (jnp.exp(d - m_t) * s).astype(bf16)
        s_inter = jnp.exp(inter_log - m_t)
        num = jnp.dot(w, v_aug, preferred_element_type=f32) + s_inter * jnp.dot(
            qh, c_prev.astype(bf16), preferred_element_type=f32)
        nq = num[:, M_V:M_V + 1]
        hv = num[:, :M_V] / jnp.maximum(jnp.abs(nq), jnp.exp(-m_t))
        hv = hv * lax.rsqrt(jnp.mean(hv * hv, axis=-1, keepdims=True) + RMS_EPS)
        gate = jax.nn.sigmoid(og_ref[:, h * M_V:(h + 1) * M_V].astype(f32))
        out_ref[:, h * M_V:(h + 1) * M_V] = (
            hv * nrm_ref[:, h * M_V:(h + 1) * M_V] * gate).astype(out_ref.dtype)

        a_col = g_tot - b_col + li_col
        m_loc = jnp.max(a_col, axis=0, keepdims=True)
        wv = (jnp.exp(a_col - m_loc) * v_aug.astype(f32)).astype(bf16)
        c_loc = jnp.dot(khT, wv, preferred_element_type=f32)
        m_new = jnp.maximum(g_tot + m_prev, m_loc)
        c_scr[h] = jnp.exp(g_tot + m_prev - m_new) * c_prev + jnp.exp(m_loc - m_new) * c_loc
        m_scr[h] = jnp.broadcast_to(m_new, (1, LANES))


def _mlstm(proj, G, GT, conv_w, conv_b, m_norm):
    return pl.pallas_call(
        _mlstm_kernel,
        grid=(NBLK,),
        in_specs=[pl.BlockSpec((None, BLK, GROUP_W), lambda i: (0, i, 0)),
                  pl.BlockSpec((None, BLK, GROUP_W), lambda i: (1, i, 0)),
                  pl.BlockSpec((None, BLK, GROUP_W), lambda i: (2, i, 0)),
                  pl.BlockSpec((BLK, LANES), lambda i: (i, 0)),
                  pl.BlockSpec((8, BLK), lambda i: (0, i)),
                  pl.BlockSpec((CONV_W, 2 * M_QK_W), lambda i: (0, 0)),
                  pl.BlockSpec((1, 2 * M_QK_W), lambda i: (0, 0)),
                  pl.BlockSpec((1, M_V_W), lambda i: (0, 0))],
        out_specs=pl.BlockSpec((BLK, M_V_W), lambda i: (i, 0)),
        out_shape=jax.ShapeDtypeStruct((LP, M_V_W), bf16),
        scratch_shapes=[pltpu.VMEM((M_HEADS, M_QK, C_AUG), f32),
                        pltpu.VMEM((M_HEADS, 1, LANES), f32),
                        pltpu.VMEM((1, LANES), f32),
                        pltpu.VMEM((8, 2 * M_QK_W), f32)],
        compiler_params=pltpu.CompilerParams(
            dimension_semantics=("arbitrary",), vmem_limit_bytes=VMEM_LIMIT),
        name="mlstm",
    )(proj, proj, proj, G, GT, conv_w, conv_b, m_norm)


def _fox_kernel(q_ref, k_ref, v_ref, g_ref, gt_ref, nrm_ref, o_ref, m_scr, l_scr, acc_scr):
    h = pl.program_id(0)
    i = pl.program_id(1)
    T = BLK
    q = q_ref[...]
    lane = lax.broadcasted_iota(jnp.int32, (T, LANES), 1)
    c_col = jnp.sum(jnp.where(lane == 2 * M_HEADS + h, g_ref[...], 0.0), axis=1, keepdims=True)
    qpos = i * T + lax.broadcasted_iota(jnp.int32, (T, 1), 0)
    m_scr[...] = jnp.full_like(m_scr, NEG)
    l_scr[...] = jnp.zeros_like(l_scr)
    acc_scr[...] = jnp.zeros_like(acc_scr)

    def step(j, masked):
        start = pl.multiple_of(j * T, T)
        k = k_ref[pl.ds(start, T), :]
        v = v_ref[pl.ds(start, T), :]
        c_row = gt_ref[pl.ds(h, 1), pl.ds(start, T)]
        s = lax.dot_general(q, k, (((1,), (1,)), ((), ())), preferred_element_type=f32)
        s = s + (c_col - c_row)
        if masked:
            kpos = start + lax.broadcasted_iota(jnp.int32, (1, T), 1)
            s = jnp.where((kpos <= qpos) & (kpos >= P_PAD), s, NEG)
        m_prev = m_scr[...]
        m_new = jnp.maximum(m_prev, jnp.max(s, axis=-1, keepdims=True))
        p = jnp.exp(s - m_new)
        alpha = jnp.exp(m_prev - m_new)
        l_scr[...] = alpha * l_scr[...] + jnp.sum(p, axis=-1, keepdims=True)
        acc_scr[...] = alpha * acc_scr[...] + jnp.dot(p.astype(bf16), v, preferred_element_type=f32)
        m_scr[...] = m_new

    step(0, True)

    def body(j, carry):
        step(j, False)
        return carry

    lax.fori_loop(1, i, body, 0)

    @pl.when(i > 0)
    def _():
        step(i, True)

    o = acc_scr[...] / l_scr[...]
    o = o * lax.rsqrt(jnp.mean(o * o, axis=-1, keepdims=True) + RMS_EPS)
    o_ref[...] = (o * nrm_ref[...]).astype(o_ref.dtype)


def _fox(proj, G, GT, f_norm):
    return pl.pallas_call(
        _fox_kernel,
        grid=(F_HEADS, NBLK),
        in_specs=[pl.BlockSpec((None, BLK, F_HD), lambda h, i: (3, i, h)),
                  pl.BlockSpec((None, LP, F_HD), lambda h, i: (4, 0, h)),
                  pl.BlockSpec((None, LP, F_HD), lambda h, i: (5, 0, h)),
                  pl.BlockSpec((BLK, LANES), lambda h, i: (i, 0)),
                  pl.BlockSpec((8, LP), lambda h, i: (1, 0)),
                  pl.BlockSpec((None, 1, F_HD), lambda h, i: (h, 0, 0))],
        out_specs=pl.BlockSpec((BLK, F_HD), lambda h, i: (i, h)),
        out_shape=jax.ShapeDtypeStruct((LP, F_W), bf16),
        scratch_shapes=[pltpu.VMEM((BLK, 1), f32),
                        pltpu.VMEM((BLK, 1), f32),
                        pltpu.VMEM((BLK, F_HD), f32)],
        compiler_params=pltpu.CompilerParams(
            dimension_semantics=("arbitrary", "arbitrary"), vmem_limit_bytes=VMEM_LIMIT),
        name="fox",
    )(proj, proj, proj, G, GT, f_norm)


def _outproj_kernel(hm_ref, hf_ref, w_ref, h_ref, g_ref, b_ref, of_ref, ob_ref):
    y = jnp.dot(hm_ref[...], w_ref[:M_V_W, :], preferred_element_type=f32)
    y = y + jnp.dot(hf_ref[...], w_ref[M_V_W:, :], preferred_element_type=f32)
    out = _layer_norm(DN_ALPHA * h_ref[...] + y, g_ref[...], b_ref[...])
    of_ref[...] = out
    ob_ref[...] = out.astype(bf16)


def _outproj(hm, hf, w, h, g, b):
    row = lambda i: (i, 0)
    const = lambda i: (0, 0)
    return pl.pallas_call(
        _outproj_kernel,
        grid=(LP // TM_OUT,),
        in_specs=[pl.BlockSpec((TM_OUT, M_V_W), row),
                  pl.BlockSpec((TM_OUT, F_W), row),
                  pl.BlockSpec((M_V_W + F_W, D_MODEL), const),
                  pl.BlockSpec((TM_OUT, D_MODEL), row),
                  pl.BlockSpec((1, D_MODEL), const),
                  pl.BlockSpec((1, D_MODEL), const)],
        out_specs=[pl.BlockSpec((TM_OUT, D_MODEL), row),
                   pl.BlockSpec((TM_OUT, D_MODEL), row)],
        out_shape=[jax.ShapeDtypeStruct((LP, D_MODEL), f32),
                   jax.ShapeDtypeStruct((LP, D_MODEL), bf16)],
        compiler_params=pltpu.CompilerParams(
            dimension_semantics=("arbitrary",), vmem_limit_bytes=VMEM_LIMIT),
        name="outproj",
    )(hm, hf, w, h, g, b)


def _mlp_kernel(xb_ref, wu_ref, wd_ref, h_ref, g_ref, b_ref, of_ref, ob_ref, acc_ref):
    f = pl.program_id(1)
    u = jnp.maximum(jnp.dot(xb_ref[...], wu_ref[...], preferred_element_type=f32), 0.0)
    part = jnp.dot((u * u).astype(bf16), wd_ref[...], preferred_element_type=f32)

    @pl.when(f == 0)
    def _():
        acc_ref[...] = part

    @pl.when(f > 0)
    def _():
        acc_ref[...] += part

    @pl.when(f == pl.num_programs(1) - 1)
    def _():
        out = _layer_norm(DN_ALPHA * h_ref[...] + acc_ref[...], g_ref[...], b_ref[...])
        of_ref[...] = out
        ob_ref[...] = out.astype(bf16)


def _mlp(xb, wu, wd, h, g, b):
    row = lambda i, f: (i, 0)
    const = lambda i, f: (0, 0)
    return pl.pallas_call(
        _mlp_kernel,
        grid=(LP // TM_OUT, D_FF // TF),
        in_specs=[pl.BlockSpec((TM_OUT, D_MODEL), row),
                  pl.BlockSpec((D_MODEL, TF), lambda i, f: (0, f)),
                  pl.BlockSpec((TF, D_MODEL), lambda i, f: (f, 0)),
                  pl.BlockSpec((TM_OUT, D_MODEL), row),
                  pl.BlockSpec((1, D_MODEL), const),
                  pl.BlockSpec((1, D_MODEL), const)],
        out_specs=[pl.BlockSpec((TM_OUT, D_MODEL), row),
                   pl.BlockSpec((TM_OUT, D_MODEL), row)],
        out_shape=[jax.ShapeDtypeStruct((LP, D_MODEL), f32),
                   jax.ShapeDtypeStruct((LP, D_MODEL), bf16)],
        scratch_shapes=[pltpu.VMEM((TM_OUT, D_MODEL), f32)],
        compiler_params=pltpu.CompilerParams(
            dimension_semantics=("arbitrary", "arbitrary"), vmem_limit_bytes=VMEM_LIMIT),
        name="mlp",
    )(xb, wu, wd, h, g, b)


def _split_w_in(w):
    o = 0
    mqk = w[:, o:o + 2 * M_QK_W]; o += 2 * M_QK_W
    mv = w[:, o:o + M_V_W]; o += M_V_W
    mi = w[:, o:o + M_HEADS]; o += M_HEADS
    mf = w[:, o:o + M_HEADS]; o += M_HEADS
    mo = w[:, o:o + M_V_W]; o += M_V_W
    fq = w[:, o:o + F_W]; o += F_W
    fk = w[:, o:o + F_W]; o += F_W
    fv = w[:, o:o + F_W]; o += F_W
    ff = w[:, o:o + F_HEADS]
    big = jnp.concatenate([mqk, mv, mo, fq * (F_HD ** -0.5), fk, fv], axis=1).astype(bf16)
    n_gate = 2 * M_HEADS + F_HEADS
    gate = jnp.concatenate([mi, mf, ff, jnp.zeros((D_MODEL, LANES - n_gate), f32)], axis=1)
    return big, gate


def kernel(x, meta, w_in, conv_w, conv_b, m_i_bias, m_f_bias, m_norm, f_f_bias, f_norm, w_out,
           ln1_g, ln1_b, w_up, w_down, ln2_g, ln2_b):
    assert x.shape == (1, SEQ, D_MODEL)
    h = jnp.concatenate([jnp.zeros((P_PAD, D_MODEL), f32), meta.astype(f32), x[0].astype(f32)], axis=0)
    hb = h.astype(bf16)
    n_gate = 2 * M_HEADS + F_HEADS
    for l in range(DEPTH):
        w_big, w_gate = _split_w_in(w_in[l])
        b_gate = jnp.concatenate([m_i_bias[l], m_f_bias[l], f_f_bias[l],
                                  jnp.zeros((LANES - n_gate,), f32)])[None, :]
        G, GT = _gates(h, w_gate, b_gate)
        proj = _inproj(hb, w_big)
        hm = _mlstm(proj, G, GT, conv_w[l], conv_b[l][None, :], m_norm[l][None, :])
        hf = _fox(proj, G, GT, f_norm[l].reshape(F_HEADS, 1, F_HD))
        h1, h1b = _outproj(hm, hf, w_out[l].astype(bf16), h, ln1_g[l][None, :], ln1_b[l][None, :])
        h, hb = _mlp(h1b, w_up[l].astype(bf16), w_down[l].astype(bf16), h1,
                     ln2_g[l][None, :], ln2_b[l][None, :])
    return h[P_PAD + N_META:][None]
```

```python
import functools

import jax
import jax.numpy as jnp
from jax import lax
from jax.experimental import pallas as pl
from jax.experimental.pallas import tpu as pltpu

D_MODEL = 2048
SEQ = 16384
DEPTH = 2
N_META = 16
M_HEADS = 4
M_QK = 128
M_V = 256
CONV_W = 4
F_HEADS = 8
F_HD = 128
D_FF = 4 * D_MODEL
M_QK_W = M_HEADS * M_QK
M_V_W = M_HEADS * M_V
F_W = F_HEADS * F_HD
DN_ALPHA = (2 * DEPTH) ** 0.25
LN_EPS = 1e-5
RMS_EPS = 1e-6
NEG = -1e30

LANES = 128
BLK = 256
P_PAD = BLK - N_META
LP = P_PAD + N_META + SEQ
NBLK = LP // BLK
GROUP_W = 1024
N_GROUPS = 4
N_GROUPS_T = 2
TQ = 1280
SUB = TQ // BLK
N_BIAS = 3
ONES_ROWS = 16
LOG2E = 1.4426950408889634
TM_PROJ = 1040
TM_OUT = 520
TF = 1024
C_AUG = M_V + LANES
VMEM_LIMIT = 56 * 1024 * 1024

f32 = jnp.float32
bf16 = jnp.bfloat16


def _layer_norm(z, g, b):
    mu = jnp.mean(z, axis=-1, keepdims=True)
    zc = z - mu
    var = jnp.mean(zc * zc, axis=-1, keepdims=True)
    return zc * lax.rsqrt(var + LN_EPS) * g + b


def _log_sigmoid(x):
    return jnp.minimum(x, 0.0) - jnp.log(1.0 + jnp.exp(-jnp.abs(x)))


def _bf16_piece(x):
    return x.astype(bf16).astype(f32)


def _gates_kernel(h_ref, w_ref, b_ref, g_ref, gt_ref, kb_ref, qbt_ref, carry_ref):
    i = pl.program_id(0)

    @pl.when(i == 0)
    def _():
        carry_ref[...] = jnp.zeros_like(carry_ref)

    pre = jnp.dot(h_ref[...], w_ref[...], preferred_element_type=f32,
                  precision=lax.Precision.HIGHEST) + b_ref[...]
    row = i * BLK + lax.broadcasted_iota(jnp.int32, (BLK, LANES), 0)
    col = lax.broadcasted_iota(jnp.int32, (BLK, LANES), 1)
    valid = row >= P_PAD
    is_f = (col >= M_HEADS) & (col < 2 * M_HEADS + F_HEADS)
    lf = jnp.where(valid & is_f, _log_sigmoid(pre), 0.0)
    r2 = lax.broadcasted_iota(jnp.int32, (BLK, BLK), 0)
    c2 = lax.broadcasted_iota(jnp.int32, (BLK, BLK), 1)
    tri = jnp.where(c2 <= r2, 1.0, 0.0).astype(f32)
    cum = jnp.dot(tri, lf, preferred_element_type=f32,
                  precision=lax.Precision.HIGHEST) + carry_ref[...]
    carry_ref[...] = cum[BLK - 1:BLK, :]
    out = jnp.where(col < M_HEADS, jnp.where(valid, pre, NEG), cum)
    g_ref[...] = out
    gt_ref[...] = out.T

    valid_col = i * BLK + lax.broadcasted_iota(jnp.int32, (BLK, 1), 0) >= P_PAD
    for h in range(F_HEADS):
        c = cum[:, 2 * M_HEADS + h:2 * M_HEADS + h + 1] * LOG2E
        pieces = []
        rest = c
        for _ in range(N_BIAS):
            piece = _bf16_piece(rest)
            pieces.append(piece)
            rest = rest - piece
        tile_k = jnp.where(col < N_BIAS, 1.0, 0.0)
        tile_q = jnp.where((col >= N_BIAS) & (col < 2 * N_BIAS), 1.0, 0.0)
        for n, piece in enumerate(pieces):
            key_piece = jnp.where(valid_col, -piece, NEG) if n == 0 else -piece
            tile_k = jnp.where(col == N_BIAS + n, key_piece, tile_k)
            tile_q = jnp.where(col == n, piece, tile_q)
        kb_ref[h] = tile_k.astype(bf16)
        qbt_ref[h] = tile_q.T.astype(bf16)


def _gates(h, wg, bg):
    return pl.pallas_call(
        _gates_kernel,
        grid=(NBLK,),
        in_specs=[pl.BlockSpec((BLK, D_MODEL), lambda i: (i, 0)),
                  pl.BlockSpec((D_MODEL, LANES), lambda i: (0, 0)),
                  pl.BlockSpec((1, LANES), lambda i: (0, 0))],
        out_specs=[pl.BlockSpec((BLK, LANES), lambda i: (i, 0)),
                   pl.BlockSpec((LANES, BLK), lambda i: (0, i)),
                   pl.BlockSpec((F_HEADS, BLK, LANES), lambda i: (0, i, 0)),
                   pl.BlockSpec((F_HEADS, LANES, BLK), lambda i: (0, 0, i))],
        out_shape=[jax.ShapeDtypeStruct((LP, LANES), f32),
                   jax.ShapeDtypeStruct((LANES, LP), f32),
                   jax.ShapeDtypeStruct((F_HEADS, LP, LANES), bf16),
                   jax.ShapeDtypeStruct((F_HEADS, LANES, LP), bf16)],
        scratch_shapes=[pltpu.VMEM((1, LANES), f32)],
        compiler_params=pltpu.CompilerParams(dimension_semantics=("arbitrary",)),
        name="gates",
    )(h, wg, bg)


def _inproj_kernel(x_ref, w_ref, o_ref):
    o_ref[...] = jnp.dot(x_ref[...], w_ref[...], preferred_element_type=f32).astype(o_ref.dtype)


def _inproj(xb, w):
    return pl.pallas_call(
        _inproj_kernel,
        grid=(N_GROUPS, LP // TM_PROJ),
        in_specs=[pl.BlockSpec((TM_PROJ, D_MODEL), lambda j, i: (i, 0)),
                  pl.BlockSpec((D_MODEL, GROUP_W), lambda j, i: (0, j))],
        out_specs=pl.BlockSpec((None, TM_PROJ, GROUP_W), lambda j, i: (j, i, 0)),
        out_shape=jax.ShapeDtypeStruct((N_GROUPS, LP, GROUP_W), bf16),
        compiler_params=pltpu.CompilerParams(
            dimension_semantics=("arbitrary", "arbitrary"), vmem_limit_bytes=VMEM_LIMIT),
        name="inproj",
    )(xb, w)


def _inproj_t_kernel(x_ref, wt_ref, o_ref):
    o_ref[...] = lax.dot_general(wt_ref[...], x_ref[...], (((1,), (1,)), ((), ())),
                                 preferred_element_type=f32).astype(o_ref.dtype)


def _inproj_t(xb, wt):
    return pl.pallas_call(
        _inproj_t_kernel,
        grid=(N_GROUPS_T, LP // TQ),
        in_specs=[pl.BlockSpec((TQ, D_MODEL), lambda j, i: (i, 0)),
                  pl.BlockSpec((None, GROUP_W, D_MODEL), lambda j, i: (j, 0, 0))],
        out_specs=pl.BlockSpec((None, GROUP_W, TQ), lambda j, i: (j, 0, i)),
        out_shape=jax.ShapeDtypeStruct((N_GROUPS_T, GROUP_W, LP), bf16),
        compiler_params=pltpu.CompilerParams(
            dimension_semantics=("arbitrary", "arbitrary"), vmem_limit_bytes=VMEM_LIMIT),
        name="inproj_t",
    )(xb, wt)


def _mlstm_kernel(qk_ref, v_ref, og_ref, g_ref, gt_ref, cw_ref, cb_ref, nrm_ref, out_ref,
                  c_scr, m_scr, cf_scr, conv_scr):
    i = pl.program_id(0)
    T = BLK

    @pl.when(i == 0)
    def _():
        c_scr[...] = jnp.zeros_like(c_scr)
        m_scr[...] = jnp.zeros_like(m_scr)
        cf_scr[...] = jnp.zeros_like(cf_scr)
        conv_scr[...] = jnp.zeros_like(conv_scr)

    row = i * T + lax.broadcasted_iota(jnp.int32, (T, 1), 0)
    x = jnp.where(row >= P_PAD, qk_ref[...].astype(f32), 0.0)
    xs = jnp.concatenate([conv_scr[...], x], axis=0)
    cw = cw_ref[...]
    conv = cb_ref[...]
    for k in range(CONV_W):
        off = 8 - (CONV_W - 1) + k
        conv = conv + cw[k:k + 1, :] * xs[off:off + T, :]
    conv_scr[...] = x[T - 8:T, :]
    act = conv * jax.nn.sigmoid(conv)

    G = g_ref[...]
    GT = gt_ref[...]
    cfs_all = cf_scr[...]
    cf_scr[...] = G[T - 1:T, :]
    r2 = lax.broadcasted_iota(jnp.int32, (T, T), 0)
    c2 = lax.broadcasted_iota(jnp.int32, (T, T), 1)
    causal = c2 <= r2
    ones_blk = jnp.where(lax.broadcasted_iota(jnp.int32, (T, LANES), 1) == 0, 1.0, 0.0).astype(bf16)

    for h in range(M_HEADS):
        qh = (act[:, h * M_QK:(h + 1) * M_QK] * (M_QK ** -0.5)).astype(bf16)
        kf = act[:, M_QK_W + h * M_QK:M_QK_W + (h + 1) * M_QK]
        kh = kf.astype(bf16)
        khT = kf.T.astype(bf16)
        v_aug = jnp.concatenate([v_ref[:, h * M_V:(h + 1) * M_V], ones_blk], axis=1)
        cfs = cfs_all[:, M_HEADS + h:M_HEADS + h + 1]
        b_col = G[:, M_HEADS + h:M_HEADS + h + 1] - cfs
        b_row = GT[M_HEADS + h:M_HEADS + h + 1, :] - cfs
        li_col = G[:, h:h + 1]
        li_row = GT[h:h + 1, :]
        g_tot = b_col[T - 1:T, :]
        m_prev = m_scr[h][:, 0:1]
        c_prev = c_scr[h]

        d = jnp.where(causal, b_col - b_row + li_row, NEG)
        inter_log = b_col + m_prev
        m_t = jnp.maximum(inter_log, jnp.max(d, axis=-1, keepdims=True))
        s = lax.dot_general(qh, kh, (((1,), (1,)), ((), ())), preferred_element_type=f32)
        w = (jnp.exp(d - m_t) * s).astype(bf16)
        s_inter = jnp.exp(inter_log - m_t)
        num = jnp.dot(w, v_aug, preferred_element_type=f32) + s_inter * jnp.dot(
            qh, c_prev.astype(bf16), preferred_element_type=f32)
        nq = num[:, M_V:M_V + 1]
        hv = num[:, :M_V] / jnp.maximum(jnp.abs(nq), jnp.exp(-m_t))
        hv = hv * lax.rsqrt(jnp.mean(hv * hv, axis=-1, keepdims=True) + RMS_EPS)
        gate = jax.nn.sigmoid(og_ref[:, h * M_V:(h + 1) * M_V].astype(f32))
        out_ref[:, h * M_V:(h + 1) * M_V] = (
            hv * nrm_ref[:, h * M_V:(h + 1) * M_V] * gate).astype(out_ref.dtype)

        a_col = g_tot - b_col + li_col
        m_loc = jnp.max(a_col, axis=0, keepdims=True)
        wv = (jnp.exp(a_col - m_loc) * v_aug.astype(f32)).astype(bf16)
        c_loc = jnp.dot(khT, wv, preferred_element_type=f32)
        m_new = jnp.maximum(g_tot + m_prev, m_loc)
        c_scr[h] = jnp.exp(g_tot + m_prev - m_new) * c_prev + jnp.exp(m_loc - m_new) * c_loc
        m_scr[h] = jnp.broadcast_to(m_new, (1, LANES))


def _mlstm(proj, G, GT, conv_w, conv_b, m_norm):
    return pl.pallas_call(
        _mlstm_kernel,
        grid=(NBLK,),
        in_specs=[pl.BlockSpec((None, BLK, GROUP_W), lambda i: (0, i, 0)),
                  pl.BlockSpec((None, BLK, GROUP_W), lambda i: (1, i, 0)),
                  pl.BlockSpec((None, BLK, GROUP_W), lambda i: (2, i, 0)),
                  pl.BlockSpec((BLK, LANES), lambda i: (i, 0)),
                  pl.BlockSpec((8, BLK), lambda i: (0, i)),
                  pl.BlockSpec((CONV_W, 2 * M_QK_W), lambda i: (0, 0)),
                  pl.BlockSpec((1, 2 * M_QK_W), lambda i: (0, 0)),
                  pl.BlockSpec((1, M_V_W), lambda i: (0, 0))],
        out_specs=pl.BlockSpec((BLK, M_V_W), lambda i: (i, 0)),
        out_shape=jax.ShapeDtypeStruct((LP, M_V_W), bf16),
        scratch_shapes=[pltpu.VMEM((M_HEADS, M_QK, C_AUG), f32),
                        pltpu.VMEM((M_HEADS, 1, LANES), f32),
                        pltpu.VMEM((1, LANES), f32),
                        pltpu.VMEM((8, 2 * M_QK_W), f32)],
        compiler_params=pltpu.CompilerParams(
            dimension_semantics=("arbitrary",), vmem_limit_bytes=VMEM_LIMIT),
        name="mlstm",
    )(proj, proj, proj, G, GT, conv_w, conv_b, m_norm)


def _fox_kernel(qt_ref, qbt_ref, k_ref, kb_ref, vt_ref, nrm_ref, o_ref, m_scr, acc_scr):
    qi = pl.program_id(1)
    q_aug = jnp.concatenate([qt_ref[...], qbt_ref[...]], axis=0)
    m_scr[...] = jnp.full_like(m_scr, NEG)
    acc_scr[...] = jnp.zeros_like(acc_scr)
    ones_rows = jnp.ones((ONES_ROWS, BLK), bf16)

    def step(j, q_lo, masked):
        start = pl.multiple_of(j * BLK, BLK)
        k_aug = jnp.concatenate([k_ref[pl.ds(start, BLK), :], kb_ref[pl.ds(start, BLK), :]], axis=1)
        v_aug = jnp.concatenate([vt_ref[:, pl.ds(start, BLK)], ones_rows], axis=0)
        s = jnp.dot(k_aug, q_aug[:, q_lo:], preferred_element_type=f32)
        if masked:
            kpos = start + lax.broadcasted_iota(jnp.int32, (BLK, 1), 0)
            qpos = qi * TQ + q_lo + lax.broadcasted_iota(jnp.int32, (1, TQ - q_lo), 1)
            s = jnp.where(kpos <= qpos, s, NEG)
        m_prev = m_scr[:, q_lo:]
        m_new = jnp.maximum(m_prev, jnp.max(s, axis=0, keepdims=True))
        p = jnp.exp2(s - m_new).astype(bf16)
        alpha = jnp.exp2(m_prev - m_new)
        acc_scr[:, q_lo:] = alpha * acc_scr[:, q_lo:] + jnp.dot(v_aug, p, preferred_element_type=f32)
        m_scr[:, q_lo:] = m_new

    def body(j, carry):
        step(j, 0, False)
        return carry

    lax.fori_loop(0, qi * SUB, body, 0)
    for d in range(SUB):
        step(qi * SUB + d, d * BLK, True)

    acc = acc_scr[...]
    ot = acc[:F_HD, :] / acc[F_HD:F_HD + 1, :]
    ot = ot * lax.rsqrt(jnp.mean(ot * ot, axis=0, keepdims=True) + RMS_EPS)
    o_ref[...] = (ot.T * nrm_ref[...]).astype(o_ref.dtype)


def _fox(proj, proj_t, kb, qbt, f_norm):
    return pl.pallas_call(
        _fox_kernel,
        grid=(F_HEADS, LP // TQ),
        in_specs=[pl.BlockSpec((None, F_HD, TQ), lambda h, i: (0, h, i)),
                  pl.BlockSpec((None, LANES, TQ), lambda h, i: (h, 0, i)),
                  pl.BlockSpec((None, LP, F_HD), lambda h, i: (3, 0, h)),
                  pl.BlockSpec((None, LP, LANES), lambda h, i: (h, 0, 0)),
                  pl.BlockSpec((None, F_HD, LP), lambda h, i: (1, h, 0)),
                  pl.BlockSpec((None, 1, F_HD), lambda h, i: (h, 0, 0))],
        out_specs=pl.BlockSpec((TQ, F_HD), lambda h, i: (i, h)),
        out_shape=jax.ShapeDtypeStruct((LP, F_W), bf16),
        scratch_shapes=[pltpu.VMEM((1, TQ), f32),
                        pltpu.VMEM((F_HD + ONES_ROWS, TQ), f32)],
        compiler_params=pltpu.CompilerParams(
            dimension_semantics=("arbitrary", "arbitrary"), vmem_limit_bytes=VMEM_LIMIT),
        name="fox",
    )(proj_t, qbt, proj, kb, proj_t, f_norm)


def _outproj_kernel(hm_ref, hf_ref, w_ref, h_ref, g_ref, b_ref, of_ref, ob_ref):
    y = jnp.dot(hm_ref[...], w_ref[:M_V_W, :], preferred_element_type=f32)
    y = y + jnp.dot(hf_ref[...], w_ref[M_V_W:, :], preferred_element_type=f32)
    out = _layer_norm(DN_ALPHA * h_ref[...] + y, g_ref[...], b_ref[...])
    of_ref[...] = out
    ob_ref[...] = out.astype(bf16)


def _outproj(hm, hf, w, h, g, b):
    row = lambda i: (i, 0)
    const = lambda i: (0, 0)
    return pl.pallas_call(
        _outproj_kernel,
        grid=(LP // TM_OUT,),
        in_specs=[pl.BlockSpec((TM_OUT, M_V_W), row),
                  pl.BlockSpec((TM_OUT, F_W), row),
                  pl.BlockSpec((M_V_W + F_W, D_MODEL), const),
                  pl.BlockSpec((TM_OUT, D_MODEL), row),
                  pl.BlockSpec((1, D_MODEL), const),
                  pl.BlockSpec((1, D_MODEL), const)],
        out_specs=[pl.BlockSpec((TM_OUT, D_MODEL), row),
                   pl.BlockSpec((TM_OUT, D_MODEL), row)],
        out_shape=[jax.ShapeDtypeStruct((LP, D_MODEL), f32),
                   jax.ShapeDtypeStruct((LP, D_MODEL), bf16)],
        compiler_params=pltpu.CompilerParams(
            dimension_semantics=("arbitrary",), vmem_limit_bytes=VMEM_LIMIT),
        name="outproj",
    )(hm, hf, w, h, g, b)


def _mlp_kernel(xb_ref, wu_ref, wd_ref, h_ref, g_ref, b_ref, of_ref, ob_ref, acc_ref):
    f = pl.program_id(1)

    @pl.when(f == 0)
    def _():
        acc_ref[...] = jnp.zeros_like(acc_ref)

    u = jnp.maximum(jnp.dot(xb_ref[...], wu_ref[...], preferred_element_type=f32), 0.0)
    acc_ref[...] += jnp.dot((u * u).astype(bf16), wd_ref[...], preferred_element_type=f32)

    @pl.when(f == pl.num_programs(1) - 1)
    def _():
        out = _layer_norm(DN_ALPHA * h_ref[...] + acc_ref[...], g_ref[...], b_ref[...])
        of_ref[...] = out
        ob_ref[...] = out.astype(bf16)


def _mlp(xb, wu, wd, h, g, b):
    row = lambda i, f: (i, 0)
    const = lambda i, f: (0, 0)
    return pl.pallas_call(
        _mlp_kernel,
        grid=(LP // TM_OUT, D_FF // TF),
        in_specs=[pl.BlockSpec((TM_OUT, D_MODEL), row),
                  pl.BlockSpec((D_MODEL, TF), lambda i, f: (0, f)),
                  pl.BlockSpec((TF, D_MODEL), lambda i, f: (f, 0)),
                  pl.BlockSpec((TM_OUT, D_MODEL), row),
                  pl.BlockSpec((1, D_MODEL), const),
                  pl.BlockSpec((1, D_MODEL), const)],
        out_specs=[pl.BlockSpec((TM_OUT, D_MODEL), row),
                   pl.BlockSpec((TM_OUT, D_MODEL), row)],
        out_shape=[jax.ShapeDtypeStruct((LP, D_MODEL), f32),
                   jax.ShapeDtypeStruct((LP, D_MODEL), bf16)],
        scratch_shapes=[pltpu.VMEM((TM_OUT, D_MODEL), f32)],
        compiler_params=pltpu.CompilerParams(
            dimension_semantics=("arbitrary", "arbitrary"), vmem_limit_bytes=VMEM_LIMIT),
        name="mlp",
    )(xb, wu, wd, h, g, b)


def _split_w_in(w):
    o = 0
    mqk = w[:, o:o + 2 * M_QK_W]; o += 2 * M_QK_W
    mv = w[:, o:o + M_V_W]; o += M_V_W
    mi = w[:, o:o + M_HEADS]; o += M_HEADS
    mf = w[:, o:o + M_HEADS]; o += M_HEADS
    mo = w[:, o:o + M_V_W]; o += M_V_W
    fq = w[:, o:o + F_W]; o += F_W
    fk = w[:, o:o + F_W]; o += F_W
    fv = w[:, o:o + F_W]; o += F_W
    ff = w[:, o:o + F_HEADS]
    big = jnp.concatenate([mqk, mv, mo, fk], axis=1).astype(bf16)
    big_t = jnp.stack([fq.T * (F_HD ** -0.5 * LOG2E), fv.T]).astype(bf16)
    n_gate = 2 * M_HEADS + F_HEADS
    gate = jnp.concatenate([mi, mf, ff, jnp.zeros((D_MODEL, LANES - n_gate), f32)], axis=1)
    return big, big_t, gate


def kernel(x, meta, w_in, conv_w, conv_b, m_i_bias, m_f_bias, m_norm, f_f_bias, f_norm, w_out,
           ln1_g, ln1_b, w_up, w_down, ln2_g, ln2_b):
    assert x.shape == (1, SEQ, D_MODEL)
    h = jnp.concatenate([jnp.zeros((P_PAD, D_MODEL), f32), meta.astype(f32), x[0].astype(f32)], axis=0)
    hb = h.astype(bf16)
    n_gate = 2 * M_HEADS + F_HEADS
    for l in range(DEPTH):
        w_big, w_big_t, w_gate = _split_w_in(w_in[l])
        b_gate = jnp.concatenate([m_i_bias[l], m_f_bias[l], f_f_bias[l],
                                  jnp.zeros((LANES - n_gate,), f32)])[None, :]
        G, GT, kb, qbt = _gates(h, w_gate, b_gate)
        proj = _inproj(hb, w_big)
        proj_t = _inproj_t(hb, w_big_t)
        hm = _mlstm(proj, G, GT, conv_w[l], conv_b[l][None, :], m_norm[l][None, :])
        hf = _fox(proj, proj_t, kb, qbt, f_norm[l].reshape(F_HEADS, 1, F_HD))
        h1, h1b = _outproj(hm, hf, w_out[l].astype(bf16), h, ln1_g[l][None, :], ln1_b[l][None, :])
        h, hb = _mlp(h1b, w_up[l].astype(bf16), w_down[l].astype(bf16), h1,
                     ln2_g[l][None, :], ln2_b[l][None, :])
    return h[P_PAD + N_META:][None]
```

```python
import functools

import jax
import jax.numpy as jnp
from jax import lax
from jax.experimental import pallas as pl
from jax.experimental.pallas import tpu as pltpu

D_MODEL = 2048
SEQ = 16384
DEPTH = 2
N_META = 16
M_HEADS = 4
M_QK = 128
M_V = 256
CONV_W = 4
F_HEADS = 8
F_HD = 128
D_FF = 4 * D_MODEL
M_QK_W = M_HEADS * M_QK
M_V_W = M_HEADS * M_V
F_W = F_HEADS * F_HD
DN_ALPHA = (2 * DEPTH) ** 0.25
LN_EPS = 1e-5
RMS_EPS = 1e-6
NEG = -1e30

LANES = 128
BLK = 256
P_PAD = BLK - N_META
LP = P_PAD + N_META + SEQ
NBLK = LP // BLK
GROUP_W = 1024
N_GROUPS = 4
N_GROUPS_T = 2
TQ = 1280
SUB = TQ // BLK
N_BIAS = 3
ONES_ROWS = 16
LOOKAHEAD = 5
LOG2E = 1.4426950408889634
TM_PROJ = 1040
TM_OUT = 520
TF = 1024
C_AUG = M_V + LANES
VMEM_LIMIT = 56 * 1024 * 1024

f32 = jnp.float32
bf16 = jnp.bfloat16


def _layer_norm(z, g, b):
    mu = jnp.mean(z, axis=-1, keepdims=True)
    zc = z - mu
    var = jnp.mean(zc * zc, axis=-1, keepdims=True)
    return zc * lax.rsqrt(var + LN_EPS) * g + b


def _log_sigmoid(x):
    return jnp.minimum(x, 0.0) - jnp.log(1.0 + jnp.exp(-jnp.abs(x)))


def _bf16_piece(x):
    return x.astype(bf16).astype(f32)


def _gates_kernel(h_ref, w_ref, b_ref, g_ref, gt_ref, kb_ref, qbt_ref, carry_ref):
    i = pl.program_id(0)

    @pl.when(i == 0)
    def _():
        carry_ref[...] = jnp.zeros_like(carry_ref)

    pre = jnp.dot(h_ref[...], w_ref[...], preferred_element_type=f32,
                  precision=lax.Precision.HIGHEST) + b_ref[...]
    row = i * BLK + lax.broadcasted_iota(jnp.int32, (BLK, LANES), 0)
    col = lax.broadcasted_iota(jnp.int32, (BLK, LANES), 1)
    valid = row >= P_PAD
    is_f = (col >= M_HEADS) & (col < 2 * M_HEADS + F_HEADS)
    lf = jnp.where(valid & is_f, _log_sigmoid(pre), 0.0)
    r2 = lax.broadcasted_iota(jnp.int32, (BLK, BLK), 0)
    c2 = lax.broadcasted_iota(jnp.int32, (BLK, BLK), 1)
    tri = jnp.where(c2 <= r2, 1.0, 0.0).astype(f32)
    cum = jnp.dot(tri, lf, preferred_element_type=f32,
                  precision=lax.Precision.HIGHEST) + carry_ref[...]
    carry_ref[...] = cum[BLK - 1:BLK, :]
    out = jnp.where(col < M_HEADS, jnp.where(valid, pre, NEG), cum)
    g_ref[...] = out
    gt_ref[...] = out.T

    valid_col = i * BLK + lax.broadcasted_iota(jnp.int32, (BLK, 1), 0) >= P_PAD
    for h in range(F_HEADS):
        c = cum[:, 2 * M_HEADS + h:2 * M_HEADS + h + 1] * LOG2E
        pieces = []
        rest = c
        for _ in range(N_BIAS):
            piece = _bf16_piece(rest)
            pieces.append(piece)
            rest = rest - piece
        tile_k = jnp.where(col < N_BIAS, 1.0, 0.0)
        tile_q = jnp.where((col >= N_BIAS) & (col < 2 * N_BIAS), 1.0, 0.0)
        for n, piece in enumerate(pieces):
            key_piece = jnp.where(valid_col, -piece, NEG) if n == 0 else -piece
            tile_k = jnp.where(col == N_BIAS + n, key_piece, tile_k)
            tile_q = jnp.where(col == n, piece, tile_q)
        kb_ref[h] = tile_k.astype(bf16)
        qbt_ref[h] = tile_q.T.astype(bf16)


def _gates(h, wg, bg):
    return pl.pallas_call(
        _gates_kernel,
        grid=(NBLK,),
        in_specs=[pl.BlockSpec((BLK, D_MODEL), lambda i: (i, 0)),
                  pl.BlockSpec((D_MODEL, LANES), lambda i: (0, 0)),
                  pl.BlockSpec((1, LANES), lambda i: (0, 0))],
        out_specs=[pl.BlockSpec((BLK, LANES), lambda i: (i, 0)),
                   pl.BlockSpec((LANES, BLK), lambda i: (0, i)),
                   pl.BlockSpec((F_HEADS, BLK, LANES), lambda i: (0, i, 0)),
                   pl.BlockSpec((F_HEADS, LANES, BLK), lambda i: (0, 0, i))],
        out_shape=[jax.ShapeDtypeStruct((LP, LANES), f32),
                   jax.ShapeDtypeStruct((LANES, LP), f32),
                   jax.ShapeDtypeStruct((F_HEADS, LP, LANES), bf16),
                   jax.ShapeDtypeStruct((F_HEADS, LANES, LP), bf16)],
        scratch_shapes=[pltpu.VMEM((1, LANES), f32)],
        compiler_params=pltpu.CompilerParams(dimension_semantics=("arbitrary",)),
        name="gates",
    )(h, wg, bg)


def _inproj_kernel(x_ref, w_ref, o_ref):
    o_ref[...] = jnp.dot(x_ref[...], w_ref[...], preferred_element_type=f32).astype(o_ref.dtype)


def _inproj(xb, w):
    return pl.pallas_call(
        _inproj_kernel,
        grid=(N_GROUPS, LP // TM_PROJ),
        in_specs=[pl.BlockSpec((TM_PROJ, D_MODEL), lambda j, i: (i, 0)),
                  pl.BlockSpec((D_MODEL, GROUP_W), lambda j, i: (0, j))],
        out_specs=pl.BlockSpec((None, TM_PROJ, GROUP_W), lambda j, i: (j, i, 0)),
        out_shape=jax.ShapeDtypeStruct((N_GROUPS, LP, GROUP_W), bf16),
        compiler_params=pltpu.CompilerParams(
            dimension_semantics=("arbitrary", "arbitrary"), vmem_limit_bytes=VMEM_LIMIT),
        name="inproj",
    )(xb, w)


def _inproj_t_kernel(x_ref, wt_ref, o_ref):
    o_ref[...] = lax.dot_general(wt_ref[...], x_ref[...], (((1,), (1,)), ((), ())),
                                 preferred_element_type=f32).astype(o_ref.dtype)


def _inproj_t(xb, wt):
    return pl.pallas_call(
        _inproj_t_kernel,
        grid=(N_GROUPS_T, LP // TQ),
        in_specs=[pl.BlockSpec((TQ, D_MODEL), lambda j, i: (i, 0)),
                  pl.BlockSpec((None, GROUP_W, D_MODEL), lambda j, i: (j, 0, 0))],
        out_specs=pl.BlockSpec((None, GROUP_W, TQ), lambda j, i: (j, 0, i)),
        out_shape=jax.ShapeDtypeStruct((N_GROUPS_T, GROUP_W, LP), bf16),
        compiler_params=pltpu.CompilerParams(
            dimension_semantics=("arbitrary", "arbitrary"), vmem_limit_bytes=VMEM_LIMIT),
        name="inproj_t",
    )(xb, wt)


def _mlstm_kernel(qk_ref, v_ref, og_ref, g_ref, gt_ref, cw_ref, cb_ref, nrm_ref, out_ref,
                  c_scr, m_scr, cf_scr, conv_scr):
    i = pl.program_id(0)
    T = BLK

    @pl.when(i == 0)
    def _():
        c_scr[...] = jnp.zeros_like(c_scr)
        m_scr[...] = jnp.zeros_like(m_scr)
        cf_scr[...] = jnp.zeros_like(cf_scr)
        conv_scr[...] = jnp.zeros_like(conv_scr)

    row = i * T + lax.broadcasted_iota(jnp.int32, (T, 1), 0)
    x = jnp.where(row >= P_PAD, qk_ref[...].astype(f32), 0.0)
    xs = jnp.concatenate([conv_scr[...], x], axis=0)
    cw = cw_ref[...]
    conv = cb_ref[...]
    for k in range(CONV_W):
        off = 8 - (CONV_W - 1) + k
        conv = conv + cw[k:k + 1, :] * xs[off:off + T, :]
    conv_scr[...] = x[T - 8:T, :]
    act = conv * jax.nn.sigmoid(conv)

    G = g_ref[...]
    GT = gt_ref[...]
    cfs_all = cf_scr[...]
    cf_scr[...] = G[T - 1:T, :]
    r2 = lax.broadcasted_iota(jnp.int32, (T, T), 0)
    c2 = lax.broadcasted_iota(jnp.int32, (T, T), 1)
    causal = c2 <= r2
    ones_blk = jnp.where(lax.broadcasted_iota(jnp.int32, (T, LANES), 1) == 0, 1.0, 0.0).astype(bf16)

    for h in range(M_HEADS):
        qh = (act[:, h * M_QK:(h + 1) * M_QK] * (M_QK ** -0.5)).astype(bf16)
        kf = act[:, M_QK_W + h * M_QK:M_QK_W + (h + 1) * M_QK]
        kh = kf.astype(bf16)
        khT = kf.T.astype(bf16)
        v_aug = jnp.concatenate([v_ref[:, h * M_V:(h + 1) * M_V], ones_blk], axis=1)
        cfs = cfs_all[:, M_HEADS + h:M_HEADS + h + 1]
        b_col = G[:, M_HEADS + h:M_HEADS + h + 1] - cfs
        b_row = GT[M_HEADS + h:M_HEADS + h + 1, :] - cfs
        li_col = G[:, h:h + 1]
        li_row = GT[h:h + 1, :]
        g_tot = b_col[T - 1:T, :]
        m_prev = m_scr[h][:, 0:1]
        c_prev = c_scr[h]

        d = jnp.where(causal, b_col - b_row + li_row, NEG)
        inter_log = b_col + m_prev
        m_t = jnp.maximum(inter_log, jnp.max(d, axis=-1, keepdims=True))
        s = lax.dot_general(qh, kh, (((1,), (1,)), ((), ())), preferred_element_type=f32)
        w = (jnp.exp(d - m_t) * s).astype(bf16)
        s_inter = jnp.exp(inter_log - m_t)
        num = jnp.dot(w, v_aug, preferred_element_type=f32) + s_inter * jnp.dot(
            qh, c_prev.astype(bf16), preferred_element_type=f32)
        nq = num[:, M_V:M_V + 1]
        hv = num[:, :M_V] / jnp.maximum(jnp.abs(nq), jnp.exp(-m_t))
        hv = hv * lax.rsqrt(jnp.mean(hv * hv, axis=-1, keepdims=True) + RMS_EPS)
        gate = jax.nn.sigmoid(og_ref[:, h * M_V:(h + 1) * M_V].astype(f32))
        out_ref[:, h * M_V:(h + 1) * M_V] = (
            hv * nrm_ref[:, h * M_V:(h + 1) * M_V] * gate).astype(out_ref.dtype)

        a_col = g_tot - b_col + li_col
        m_loc = jnp.max(a_col, axis=0, keepdims=True)
        wv = (jnp.exp(a_col - m_loc) * v_aug.astype(f32)).astype(bf16)
        c_loc = jnp.dot(khT, wv, preferred_element_type=f32)
        m_new = jnp.maximum(g_tot + m_prev, m_loc)
        c_scr[h] = jnp.exp(g_tot + m_prev - m_new) * c_prev + jnp.exp(m_loc - m_new) * c_loc
        m_scr[h] = jnp.broadcast_to(m_new, (1, LANES))


def _mlstm(proj, G, GT, conv_w, conv_b, m_norm):
    return pl.pallas_call(
        _mlstm_kernel,
        grid=(NBLK,),
        in_specs=[pl.BlockSpec((None, BLK, GROUP_W), lambda i: (0, i, 0)),
                  pl.BlockSpec((None, BLK, GROUP_W), lambda i: (1, i, 0)),
                  pl.BlockSpec((None, BLK, GROUP_W), lambda i: (2, i, 0)),
                  pl.BlockSpec((BLK, LANES), lambda i: (i, 0)),
                  pl.BlockSpec((8, BLK), lambda i: (0, i)),
                  pl.BlockSpec((CONV_W, 2 * M_QK_W), lambda i: (0, 0)),
                  pl.BlockSpec((1, 2 * M_QK_W), lambda i: (0, 0)),
                  pl.BlockSpec((1, M_V_W), lambda i: (0, 0))],
        out_specs=pl.BlockSpec((BLK, M_V_W), lambda i: (i, 0)),
        out_shape=jax.ShapeDtypeStruct((LP, M_V_W), bf16),
        scratch_shapes=[pltpu.VMEM((M_HEADS, M_QK, C_AUG), f32),
                        pltpu.VMEM((M_HEADS, 1, LANES), f32),
                        pltpu.VMEM((1, LANES), f32),
                        pltpu.VMEM((8, 2 * M_QK_W), f32)],
        compiler_params=pltpu.CompilerParams(
            dimension_semantics=("arbitrary",), vmem_limit_bytes=VMEM_LIMIT),
        name="mlstm",
    )(proj, proj, proj, G, GT, conv_w, conv_b, m_norm)


def _fox_kernel(qt_ref, qbt_ref, k_ref, kb_ref, vt_ref, nrm_ref, o_ref, m_scr, acc_scr):
    qi = pl.program_id(1)
    q_aug = jnp.concatenate([qt_ref[...], qbt_ref[...]], axis=0)
    m_scr[...] = jnp.full_like(m_scr, NEG)
    acc_scr[...] = jnp.zeros_like(acc_scr)

    ones_rows = jnp.ones((ONES_ROWS, BLK), bf16)

    def scores(c, start, masked):
        cols = slice(c * BLK, (c + 1) * BLK)
        k_aug = jnp.concatenate([k_ref[pl.ds(start, BLK), :], kb_ref[pl.ds(start, BLK), :]], axis=1)
        s = jnp.dot(k_aug, q_aug[:, cols], preferred_element_type=f32)
        if masked:
            kpos = start + lax.broadcasted_iota(jnp.int32, (BLK, 1), 0)
            qpos = qi * TQ + c * BLK + lax.broadcasted_iota(jnp.int32, (1, BLK), 1)
            s = jnp.where(kpos <= qpos, s, NEG)
        m_prev = m_scr[:, cols]
        m_new = jnp.maximum(m_prev, jnp.max(s, axis=0, keepdims=True))
        m_scr[:, cols] = m_new
        return s, m_prev, m_new

    def accumulate(c, start, s, m_prev, m_new):
        cols = slice(c * BLK, (c + 1) * BLK)
        v_aug = jnp.concatenate([vt_ref[:, pl.ds(start, BLK)], ones_rows], axis=0)
        p = jnp.exp2(s - m_new).astype(bf16)
        alpha = jnp.exp2(m_prev - m_new)
        acc_scr[:, cols] = alpha * acc_scr[:, cols] + jnp.dot(v_aug, p, preferred_element_type=f32)

    def run(units, masked):
        pending = []
        for c, start in units:
            pending.append((c, start) + scores(c, start, masked))
            if len(pending) > LOOKAHEAD:
                accumulate(*pending.pop(0))
        for unit in pending:
            accumulate(*unit)

    def body(j, carry):
        start = pl.multiple_of(j * TQ, TQ)
        run([(c, pl.multiple_of(start + d * BLK, BLK)) for d in range(SUB) for c in range(SUB)], False)
        return carry

    lax.fori_loop(0, qi, body, 0)
    tail = pl.multiple_of(qi * TQ, TQ)
    run([(c, pl.multiple_of(tail + d * BLK, BLK)) for d in range(SUB) for c in range(d, SUB)], True)

    acc = acc_scr[...]
    ot = acc[:F_HD, :] / acc[F_HD:F_HD + 1, :]
    ot = ot * lax.rsqrt(jnp.mean(ot * ot, axis=0, keepdims=True) + RMS_EPS)
    o_ref[...] = (ot.T * nrm_ref[...]).astype(o_ref.dtype)


def _fox(proj, proj_t, kb, qbt, f_norm):
    return pl.pallas_call(
        _fox_kernel,
        grid=(F_HEADS, LP // TQ),
        in_specs=[pl.BlockSpec((None, F_HD, TQ), lambda h, i: (0, h, i)),
                  pl.BlockSpec((None, LANES, TQ), lambda h, i: (h, 0, i)),
                  pl.BlockSpec((None, LP, F_HD), lambda h, i: (3, 0, h)),
                  pl.BlockSpec((None, LP, LANES), lambda h, i: (h, 0, 0)),
                  pl.BlockSpec((None, F_HD, LP), lambda h, i: (1, h, 0)),
                  pl.BlockSpec((None, 1, F_HD), lambda h, i: (h, 0, 0))],
        out_specs=pl.BlockSpec((TQ, F_HD), lambda h, i: (i, h)),
        out_shape=jax.ShapeDtypeStruct((LP, F_W), bf16),
        scratch_shapes=[pltpu.VMEM((1, TQ), f32),
                        pltpu.VMEM((F_HD + ONES_ROWS, TQ), f32)],
        compiler_params=pltpu.CompilerParams(
            dimension_semantics=("arbitrary", "arbitrary"), vmem_limit_bytes=VMEM_LIMIT),
        name="fox",
    )(proj_t, qbt, proj, kb, proj_t, f_norm)


def _outproj_kernel(hm_ref, hf_ref, w_ref, h_ref, g_ref, b_ref, of_ref, ob_ref):
    y = jnp.dot(hm_ref[...], w_ref[:M_V_W, :], preferred_element_type=f32)
    y = y + jnp.dot(hf_ref[...], w_ref[M_V_W:, :], preferred_element_type=f32)
    out = _layer_norm(DN_ALPHA * h_ref[...] + y, g_ref[...], b_ref[...])
    of_ref[...] = out
    ob_ref[...] = out.astype(bf16)


def _outproj(hm, hf, w, h, g, b):
    row = lambda i: (i, 0)
    const = lambda i: (0, 0)
    return pl.pallas_call(
        _outproj_kernel,
        grid=(LP // TM_OUT,),
        in_specs=[pl.BlockSpec((TM_OUT, M_V_W), row),
                  pl.BlockSpec((TM_OUT, F_W), row),
                  pl.BlockSpec((M_V_W + F_W, D_MODEL), const),
                  pl.BlockSpec((TM_OUT, D_MODEL), row),
                  pl.BlockSpec((1, D_MODEL), const),
                  pl.BlockSpec((1, D_MODEL), const)],
        out_specs=[pl.BlockSpec((TM_OUT, D_MODEL), row),
                   pl.BlockSpec((TM_OUT, D_MODEL), row)],
        out_shape=[jax.ShapeDtypeStruct((LP, D_MODEL), f32),
                   jax.ShapeDtypeStruct((LP, D_MODEL), bf16)],
        compiler_params=pltpu.CompilerParams(
            dimension_semantics=("arbitrary",), vmem_limit_bytes=VMEM_LIMIT),
        name="outproj",
    )(hm, hf, w, h, g, b)


def _mlp_kernel(xb_ref, wu_ref, wd_ref, h_ref, g_ref, b_ref, of_ref, ob_ref, acc_ref):
    f = pl.program_id(1)

    @pl.when(f == 0)
    def _():
        acc_ref[...] = jnp.zeros_like(acc_ref)

    u = jnp.maximum(jnp.dot(xb_ref[...], wu_ref[...], preferred_element_type=f32), 0.0)
    acc_ref[...] += jnp.dot((u * u).astype(bf16), wd_ref[...], preferred_element_type=f32)

    @pl.when(f == pl.num_programs(1) - 1)
    def _():
        out = _layer_norm(DN_ALPHA * h_ref[...] + acc_ref[...], g_ref[...], b_ref[...])
        of_ref[...] = out
        ob_ref[...] = out.astype(bf16)


def _mlp(xb, wu, wd, h, g, b):
    row = lambda i, f: (i, 0)
    const = lambda i, f: (0, 0)
    return pl.pallas_call(
        _mlp_kernel,
        grid=(LP // TM_OUT, D_FF // TF),
        in_specs=[pl.BlockSpec((TM_OUT, D_MODEL), row),
                  pl.BlockSpec((D_MODEL, TF), lambda i, f: (0, f)),
                  pl.BlockSpec((TF, D_MODEL), lambda i, f: (f, 0)),
                  pl.BlockSpec((TM_OUT, D_MODEL), row),
                  pl.BlockSpec((1, D_MODEL), const),
                  pl.BlockSpec((1, D_MODEL), const)],
        out_specs=[pl.BlockSpec((TM_OUT, D_MODEL), row),
                   pl.BlockSpec((TM_OUT, D_MODEL), row)],
        out_shape=[jax.ShapeDtypeStruct((LP, D_MODEL), f32),
                   jax.ShapeDtypeStruct((LP, D_MODEL), bf16)],
        scratch_shapes=[pltpu.VMEM((TM_OUT, D_MODEL), f32)],
        compiler_params=pltpu.CompilerParams(
            dimension_semantics=("arbitrary", "arbitrary"), vmem_limit_bytes=VMEM_LIMIT),
        name="mlp",
    )(xb, wu, wd, h, g, b)


def _split_w_in(w):
    o = 0
    mqk = w[:, o:o + 2 * M_QK_W]; o += 2 * M_QK_W
    mv = w[:, o:o + M_V_W]; o += M_V_W
    mi = w[:, o:o + M_HEADS]; o += M_HEADS
    mf = w[:, o:o + M_HEADS]; o += M_HEADS
    mo = w[:, o:o + M_V_W]; o += M_V_W
    fq = w[:, o:o + F_W]; o += F_W
    fk = w[:, o:o + F_W]; o += F_W
    fv = w[:, o:o + F_W]; o += F_W
    ff = w[:, o:o + F_HEADS]
    big = jnp.concatenate([mqk, mv, mo, fk], axis=1).astype(bf16)
    big_t = jnp.stack([fq.T * (F_HD ** -0.5 * LOG2E), fv.T]).astype(bf16)
    n_gate = 2 * M_HEADS + F_HEADS
    gate = jnp.concatenate([mi, mf, ff, jnp.zeros((D_MODEL, LANES - n_gate), f32)], axis=1)
    return big, big_t, gate


def kernel(x, meta, w_in, conv_w, conv_b, m_i_bias, m_f_bias, m_norm, f_f_bias, f_norm, w_out,
           ln1_g, ln1_b, w_up, w_down, ln2_g, ln2_b):
    assert x.shape == (1, SEQ, D_MODEL)
    h = jnp.concatenate([jnp.zeros((P_PAD, D_MODEL), f32), meta.astype(f32), x[0].astype(f32)], axis=0)
    hb = h.astype(bf16)
    n_gate = 2 * M_HEADS + F_HEADS
    for l in range(DEPTH):
        w_big, w_big_t, w_gate = _split_w_in(w_in[l])
        b_gate = jnp.concatenate([m_i_bias[l], m_f_bias[l], f_f_bias[l],
                                  jnp.zeros((LANES - n_gate,), f32)])[None, :]
        G, GT, kb, qbt = _gates(h, w_gate, b_gate)
        proj = _inproj(hb, w_big)
        proj_t = _inproj_t(hb, w_big_t)
        hm = _mlstm(proj, G, GT, conv_w[l], conv_b[l][None, :], m_norm[l][None, :])
        hf = _fox(proj, proj_t, kb, qbt, f_norm[l].reshape(F_HEADS, 1, F_HD))
        h1, h1b = _outproj(hm, hf, w_out[l].astype(bf16), h, ln1_g[l][None, :], ln1_b[l][None, :])
        h, hb = _mlp(h1b, w_up[l].astype(bf16), w_down[l].astype(bf16), h1,
                     ln2_g[l][None, :], ln2_b[l][None, :])
    return h[P_PAD + N_META:][None]
```

```python
import functools

import jax
import jax.numpy as jnp
from jax import lax
from jax.experimental import pallas as pl
from jax.experimental.pallas import tpu as pltpu

D_MODEL = 2048
SEQ = 16384
DEPTH = 2
N_META = 16
M_HEADS = 4
M_QK = 128
M_V = 256
CONV_W = 4
F_HEADS = 8
F_HD = 128
D_FF = 4 * D_MODEL
M_QK_W = M_HEADS * M_QK
M_V_W = M_HEADS * M_V
F_W = F_HEADS * F_HD
DN_ALPHA = (2 * DEPTH) ** 0.25
LN_EPS = 1e-5
RMS_EPS = 1e-6
NEG = -1e30

LANES = 128
BLK = 256
P_PAD = BLK - N_META
LP = P_PAD + N_META + SEQ
NBLK = LP // BLK
GROUP_W = 1024
N_GROUPS = 4
N_GROUPS_T = 2
TQ = 1280
SUB = TQ // BLK
N_BIAS = 3
ONES_ROWS = 16
UNROLL = 2
LOOKAHEAD = 5
LOG2E = 1.4426950408889634
TM_PROJ = 1040
TM_OUT = 520
TF = 1024
C_AUG = M_V + LANES
VMEM_LIMIT = 56 * 1024 * 1024
CAST_BLOCK_BYTES = 8 * 1024 * 1024

f32 = jnp.float32
bf16 = jnp.bfloat16


def _layer_norm(z, g, b):
    mu = jnp.mean(z, axis=-1, keepdims=True)
    zc = z - mu
    var = jnp.mean(zc * zc, axis=-1, keepdims=True)
    return zc * lax.rsqrt(var + LN_EPS) * g + b


def _log_sigmoid(x):
    return jnp.minimum(x, 0.0) - jnp.log(1.0 + jnp.exp(-jnp.abs(x)))


def _bf16_piece(x):
    return x.astype(bf16).astype(f32)


def _gates_kernel(h_ref, w_ref, b_ref, g_ref, gt_ref, kb_ref, qbt_ref, carry_ref, wsplit_ref):
    i = pl.program_id(0)

    @pl.when(i == 0)
    def _():
        carry_ref[...] = jnp.zeros_like(carry_ref)
        w = w_ref[...]
        w_hi = _bf16_piece(w)
        wsplit_ref[:, :LANES] = w_hi.astype(bf16)
        wsplit_ref[:, LANES:] = (w - w_hi).astype(bf16)

    x = h_ref[...]
    x_hi = _bf16_piece(x)
    both = jnp.dot(x_hi.astype(bf16), wsplit_ref[...], preferred_element_type=f32)
    low = jnp.dot((x - x_hi).astype(bf16), wsplit_ref[:, :LANES], preferred_element_type=f32)
    pre = both[:, :LANES] + (both[:, LANES:] + low) + b_ref[...]
    row = i * BLK + lax.broadcasted_iota(jnp.int32, (BLK, LANES), 0)
    col = lax.broadcasted_iota(jnp.int32, (BLK, LANES), 1)
    valid = row >= P_PAD
    is_f = (col >= M_HEADS) & (col < 2 * M_HEADS + F_HEADS)
    lf = jnp.where(valid & is_f, _log_sigmoid(pre), 0.0)
    r2 = lax.broadcasted_iota(jnp.int32, (BLK, BLK), 0)
    c2 = lax.broadcasted_iota(jnp.int32, (BLK, BLK), 1)
    tri = jnp.where(c2 <= r2, 1.0, 0.0).astype(f32)
    cum = jnp.dot(tri, lf, preferred_element_type=f32,
                  precision=lax.Precision.HIGHEST) + carry_ref[...]
    carry_ref[...] = cum[BLK - 1:BLK, :]
    out = jnp.where(col < M_HEADS, jnp.where(valid, pre, NEG), cum)
    g_ref[...] = out
    gt_ref[...] = out.T

    valid_col = i * BLK + lax.broadcasted_iota(jnp.int32, (BLK, 1), 0) >= P_PAD
    for h in range(F_HEADS):
        c = cum[:, 2 * M_HEADS + h:2 * M_HEADS + h + 1] * LOG2E
        pieces = []
        rest = c
        for _ in range(N_BIAS):
            piece = _bf16_piece(rest)
            pieces.append(piece)
            rest = rest - piece
        tile_k = jnp.where(col < N_BIAS, 1.0, 0.0)
        tile_q = jnp.where((col >= N_BIAS) & (col < 2 * N_BIAS), 1.0, 0.0)
        for n, piece in enumerate(pieces):
            key_piece = jnp.where(valid_col, -piece, NEG) if n == 0 else -piece
            tile_k = jnp.where(col == N_BIAS + n, key_piece, tile_k)
            tile_q = jnp.where(col == n, piece, tile_q)
        kb_ref[h] = tile_k.astype(bf16)
        qbt_ref[h] = tile_q.T.astype(bf16)


def _gates(h, wg, bg):
    return pl.pallas_call(
        _gates_kernel,
        grid=(NBLK,),
        in_specs=[pl.BlockSpec((BLK, D_MODEL), lambda i: (i, 0)),
                  pl.BlockSpec((D_MODEL, LANES), lambda i: (0, 0)),
                  pl.BlockSpec((1, LANES), lambda i: (0, 0))],
        out_specs=[pl.BlockSpec((BLK, LANES), lambda i: (i, 0)),
                   pl.BlockSpec((LANES, BLK), lambda i: (0, i)),
                   pl.BlockSpec((F_HEADS, BLK, LANES), lambda i: (0, i, 0)),
                   pl.BlockSpec((F_HEADS, LANES, BLK), lambda i: (0, 0, i))],
        out_shape=[jax.ShapeDtypeStruct((LP, LANES), f32),
                   jax.ShapeDtypeStruct((LANES, LP), f32),
                   jax.ShapeDtypeStruct((F_HEADS, LP, LANES), bf16),
                   jax.ShapeDtypeStruct((F_HEADS, LANES, LP), bf16)],
        scratch_shapes=[pltpu.VMEM((1, LANES), f32),
                        pltpu.VMEM((D_MODEL, 2 * LANES), bf16)],
        compiler_params=pltpu.CompilerParams(dimension_semantics=("arbitrary",)),
        name="gates",
    )(h, wg, bg)


def _inproj_kernel(x_ref, w_ref, o_ref):
    o_ref[...] = jnp.dot(x_ref[...], w_ref[...], preferred_element_type=f32).astype(o_ref.dtype)


def _inproj(xb, w):
    return pl.pallas_call(
        _inproj_kernel,
        grid=(N_GROUPS, LP // TM_PROJ),
        in_specs=[pl.BlockSpec((TM_PROJ, D_MODEL), lambda j, i: (i, 0)),
                  pl.BlockSpec((D_MODEL, GROUP_W), lambda j, i: (0, j))],
        out_specs=pl.BlockSpec((None, TM_PROJ, GROUP_W), lambda j, i: (j, i, 0)),
        out_shape=jax.ShapeDtypeStruct((N_GROUPS, LP, GROUP_W), bf16),
        compiler_params=pltpu.CompilerParams(
            dimension_semantics=("arbitrary", "arbitrary"), vmem_limit_bytes=VMEM_LIMIT),
        name="inproj",
    )(xb, w)


def _inproj_t_kernel(x_ref, wt_ref, o_ref):
    o_ref[...] = lax.dot_general(wt_ref[...], x_ref[...], (((1,), (1,)), ((), ())),
                                 preferred_element_type=f32).astype(o_ref.dtype)


def _inproj_t(xb, wt):
    return pl.pallas_call(
        _inproj_t_kernel,
        grid=(N_GROUPS_T, LP // TQ),
        in_specs=[pl.BlockSpec((TQ, D_MODEL), lambda j, i: (i, 0)),
                  pl.BlockSpec((None, GROUP_W, D_MODEL), lambda j, i: (j, 0, 0))],
        out_specs=pl.BlockSpec((None, GROUP_W, TQ), lambda j, i: (j, 0, i)),
        out_shape=jax.ShapeDtypeStruct((N_GROUPS_T, GROUP_W, LP), bf16),
        compiler_params=pltpu.CompilerParams(
            dimension_semantics=("arbitrary", "arbitrary"), vmem_limit_bytes=VMEM_LIMIT),
        name="inproj_t",
    )(xb, wt)


def _mlstm_kernel(qk_ref, v_ref, og_ref, g_ref, gt_ref, cw_ref, cb_ref, nrm_ref, out_ref,
                  c_scr, m_scr, cf_scr, conv_scr):
    i = pl.program_id(0)
    T = BLK

    @pl.when(i == 0)
    def _():
        c_scr[...] = jnp.zeros_like(c_scr)
        m_scr[...] = jnp.zeros_like(m_scr)
        cf_scr[...] = jnp.zeros_like(cf_scr)
        conv_scr[...] = jnp.zeros_like(conv_scr)

    row = i * T + lax.broadcasted_iota(jnp.int32, (T, 1), 0)
    x = jnp.where(row >= P_PAD, qk_ref[...].astype(f32), 0.0)
    xs = jnp.concatenate([conv_scr[...], x], axis=0)
    cw = cw_ref[...]
    conv = cb_ref[...]
    for k in range(CONV_W):
        off = 8 - (CONV_W - 1) + k
        conv = conv + cw[k:k + 1, :] * xs[off:off + T, :]
    conv_scr[...] = x[T - 8:T, :]
    act = conv * jax.nn.sigmoid(conv)

    G = g_ref[...]
    GT = gt_ref[...]
    cfs_all = cf_scr[...]
    cf_scr[...] = G[T - 1:T, :]
    r2 = lax.broadcasted_iota(jnp.int32, (T, T), 0)
    c2 = lax.broadcasted_iota(jnp.int32, (T, T), 1)
    causal = c2 <= r2
    ones_blk = jnp.where(lax.broadcasted_iota(jnp.int32, (T, LANES), 1) == 0, 1.0, 0.0).astype(bf16)

    for h in range(M_HEADS):
        qh = (act[:, h * M_QK:(h + 1) * M_QK] * (M_QK ** -0.5)).astype(bf16)
        kf = act[:, M_QK_W + h * M_QK:M_QK_W + (h + 1) * M_QK]
        kh = kf.astype(bf16)
        khT = kf.T.astype(bf16)
        v_aug = jnp.concatenate([v_ref[:, h * M_V:(h + 1) * M_V], ones_blk], axis=1)
        cfs = cfs_all[:, M_HEADS + h:M_HEADS + h + 1]
        b_col = G[:, M_HEADS + h:M_HEADS + h + 1] - cfs
        b_row = GT[M_HEADS + h:M_HEADS + h + 1, :] - cfs
        li_col = G[:, h:h + 1]
        li_row = GT[h:h + 1, :]
        g_tot = b_col[T - 1:T, :]
        m_prev = m_scr[h][:, 0:1]
        c_prev = c_scr[h]

        d = jnp.where(causal, b_col - b_row + li_row, NEG)
        inter_log = b_col + m_prev
        m_t = jnp.maximum(inter_log, jnp.max(d, axis=-1, keepdims=True))
        s = lax.dot_general(qh, kh, (((1,), (1,)), ((), ())), preferred_element_type=f32)
        w = (jnp.exp(d - m_t) * s).astype(bf16)
        s_inter = jnp.exp(inter_log - m_t)
        num = jnp.dot(w, v_aug, preferred_element_type=f32) + s_inter * jnp.dot(
            qh, c_prev.astype(bf16), preferred_element_type=f32)
        nq = num[:, M_V:M_V + 1]
        hv = num[:, :M_V] / jnp.maximum(jnp.abs(nq), jnp.exp(-m_t))
        hv = hv * lax.rsqrt(jnp.mean(hv * hv, axis=-1, keepdims=True) + RMS_EPS)
        gate = jax.nn.sigmoid(og_ref[:, h * M_V:(h + 1) * M_V].astype(f32))
        out_ref[:, h * M_V:(h + 1) * M_V] = (
            hv * nrm_ref[:, h * M_V:(h + 1) * M_V] * gate).astype(out_ref.dtype)

        a_col = g_tot - b_col + li_col
        m_loc = jnp.max(a_col, axis=0, keepdims=True)
        wv = (jnp.exp(a_col - m_loc) * v_aug.astype(f32)).astype(bf16)
        c_loc = jnp.dot(khT, wv, preferred_element_type=f32)
        m_new = jnp.maximum(g_tot + m_prev, m_loc)
        c_scr[h] = jnp.exp(g_tot + m_prev - m_new) * c_prev + jnp.exp(m_loc - m_new) * c_loc
        m_scr[h] = jnp.broadcast_to(m_new, (1, LANES))


def _mlstm(proj, G, GT, conv_w, conv_b, m_norm):
    return pl.pallas_call(
        _mlstm_kernel,
        grid=(NBLK,),
        in_specs=[pl.BlockSpec((None, BLK, GROUP_W), lambda i: (0, i, 0)),
                  pl.BlockSpec((None, BLK, GROUP_W), lambda i: (1, i, 0)),
                  pl.BlockSpec((None, BLK, GROUP_W), lambda i: (2, i, 0)),
                  pl.BlockSpec((BLK, LANES), lambda i: (i, 0)),
                  pl.BlockSpec((8, BLK), lambda i: (0, i)),
                  pl.BlockSpec((CONV_W, 2 * M_QK_W), lambda i: (0, 0)),
                  pl.BlockSpec((1, 2 * M_QK_W), lambda i: (0, 0)),
                  pl.BlockSpec((1, M_V_W), lambda i: (0, 0))],
        out_specs=pl.BlockSpec((BLK, M_V_W), lambda i: (i, 0)),
        out_shape=jax.ShapeDtypeStruct((LP, M_V_W), bf16),
        scratch_shapes=[pltpu.VMEM((M_HEADS, M_QK, C_AUG), f32),
                        pltpu.VMEM((M_HEADS, 1, LANES), f32),
                        pltpu.VMEM((1, LANES), f32),
                        pltpu.VMEM((8, 2 * M_QK_W), f32)],
        compiler_params=pltpu.CompilerParams(
            dimension_semantics=("arbitrary",), vmem_limit_bytes=VMEM_LIMIT),
        name="mlstm",
    )(proj, proj, proj, G, GT, conv_w, conv_b, m_norm)


def _fox_kernel(qt_ref, qbt_ref, k_ref, kb_ref, vt_ref, nrm_ref, o_ref, m_scr, acc_scr):
    qi = pl.program_id(1)
    q_aug = jnp.concatenate([qt_ref[...], qbt_ref[...]], axis=0)
    m_scr[...] = jnp.full_like(m_scr, NEG)
    acc_scr[...] = jnp.zeros_like(acc_scr)

    ones_rows = jnp.ones((ONES_ROWS, BLK), bf16)

    def scores(c, start, masked):
        cols = slice(c * BLK, (c + 1) * BLK)
        k_aug = jnp.concatenate([k_ref[pl.ds(start, BLK), :], kb_ref[pl.ds(start, BLK), :]], axis=1)
        s = jnp.dot(k_aug, q_aug[:, cols], preferred_element_type=f32)
        if masked:
            kpos = start + lax.broadcasted_iota(jnp.int32, (BLK, 1), 0)
            qpos = qi * TQ + c * BLK + lax.broadcasted_iota(jnp.int32, (1, BLK), 1)
            s = jnp.where(kpos <= qpos, s, NEG)
        m_prev = m_scr[:, cols]
        m_new = jnp.maximum(m_prev, jnp.max(s, axis=0, keepdims=True))
        m_scr[:, cols] = m_new
        return s, m_prev, m_new

    def accumulate(c, start, s, m_prev, m_new):
        cols = slice(c * BLK, (c + 1) * BLK)
        v_aug = jnp.concatenate([vt_ref[:, pl.ds(start, BLK)], ones_rows], axis=0)
        p = jnp.exp2(s - m_new).astype(bf16)
        alpha = jnp.exp2(m_prev - m_new)
        acc_scr[:, cols] = alpha * acc_scr[:, cols] + jnp.dot(v_aug, p, preferred_element_type=f32)

    def run(units):
        pending = []
        for c, start, masked in units:
            pending.append((c, start) + scores(c, start, masked))
            if len(pending) > LOOKAHEAD:
                accumulate(*pending.pop(0))
        for unit in pending:
            accumulate(*unit)

    def full_blocks(first_key, n_blocks):
        run([(c, pl.multiple_of(first_key + d * BLK, BLK), False)
             for d in range(n_blocks) for c in range(SUB)])

    def body(j, carry):
        full_blocks(pl.multiple_of(j * (UNROLL * TQ), TQ), UNROLL * SUB)
        return carry

    lax.fori_loop(0, qi // UNROLL, body, 0)
    for r in range(1, UNROLL):
        @pl.when(qi % UNROLL >= r)
        def _():
            full_blocks(pl.multiple_of((qi // UNROLL * UNROLL + r - 1) * TQ, TQ), SUB)

    tail = pl.multiple_of(qi * TQ, TQ)
    run([(c, pl.multiple_of(tail + d * BLK, BLK), c == d) for d in range(SUB) for c in range(d, SUB)])

    acc = acc_scr[...]
    ot = acc[:F_HD, :] / acc[F_HD:F_HD + 1, :]
    ot = ot * lax.rsqrt(jnp.mean(ot * ot, axis=0, keepdims=True) + RMS_EPS)
    o_ref[...] = (ot.T * nrm_ref[...]).astype(o_ref.dtype)


def _fox(proj, proj_t, kb, qbt, f_norm):
    return pl.pallas_call(
        _fox_kernel,
        grid=(F_HEADS, LP // TQ),
        in_specs=[pl.BlockSpec((None, F_HD, TQ), lambda h, i: (0, h, i)),
                  pl.BlockSpec((None, LANES, TQ), lambda h, i: (h, 0, i)),
                  pl.BlockSpec((None, LP, F_HD), lambda h, i: (3, 0, h)),
                  pl.BlockSpec((None, LP, LANES), lambda h, i: (h, 0, 0)),
                  pl.BlockSpec((None, F_HD, LP), lambda h, i: (1, h, 0)),
                  pl.BlockSpec((None, 1, F_HD), lambda h, i: (h, 0, 0))],
        out_specs=pl.BlockSpec((TQ, F_HD), lambda h, i: (i, h)),
        out_shape=jax.ShapeDtypeStruct((LP, F_W), bf16),
        scratch_shapes=[pltpu.VMEM((1, TQ), f32),
                        pltpu.VMEM((F_HD + ONES_ROWS, TQ), f32)],
        compiler_params=pltpu.CompilerParams(
            dimension_semantics=("arbitrary", "arbitrary"), vmem_limit_bytes=VMEM_LIMIT),
        name="fox",
    )(proj_t, qbt, proj, kb, proj_t, f_norm)


def _outproj_kernel(hm_ref, hf_ref, w_ref, h_ref, g_ref, b_ref, of_ref, ob_ref):
    y = jnp.dot(hm_ref[...], w_ref[:M_V_W, :], preferred_element_type=f32)
    y = y + jnp.dot(hf_ref[...], w_ref[M_V_W:, :], preferred_element_type=f32)
    out = _layer_norm(DN_ALPHA * h_ref[...] + y, g_ref[...], b_ref[...])
    of_ref[...] = out
    ob_ref[...] = out.astype(bf16)


def _outproj(hm, hf, w, h, g, b):
    row = lambda i: (i, 0)
    const = lambda i: (0, 0)
    return pl.pallas_call(
        _outproj_kernel,
        grid=(LP // TM_OUT,),
        in_specs=[pl.BlockSpec((TM_OUT, M_V_W), row),
                  pl.BlockSpec((TM_OUT, F_W), row),
                  pl.BlockSpec((M_V_W + F_W, D_MODEL), const),
                  pl.BlockSpec((TM_OUT, D_MODEL), row),
                  pl.BlockSpec((1, D_MODEL), const),
                  pl.BlockSpec((1, D_MODEL), const)],
        out_specs=[pl.BlockSpec((TM_OUT, D_MODEL), row),
                   pl.BlockSpec((TM_OUT, D_MODEL), row)],
        out_shape=[jax.ShapeDtypeStruct((LP, D_MODEL), f32),
                   jax.ShapeDtypeStruct((LP, D_MODEL), bf16)],
        compiler_params=pltpu.CompilerParams(
            dimension_semantics=("arbitrary",), vmem_limit_bytes=VMEM_LIMIT),
        name="outproj",
    )(hm, hf, w, h, g, b)


def _mlp_kernel(xb_ref, wu_ref, wd_ref, h_ref, g_ref, b_ref, of_ref, ob_ref, acc_ref):
    f = pl.program_id(1)

    @pl.when(f == 0)
    def _():
        acc_ref[...] = jnp.zeros_like(acc_ref)

    u = jnp.maximum(jnp.dot(xb_ref[...], wu_ref[...], preferred_element_type=f32), 0.0)
    acc_ref[...] += jnp.dot((u * u).astype(bf16), wd_ref[...], preferred_element_type=f32)

    @pl.when(f == pl.num_programs(1) - 1)
    def _():
        out = _layer_norm(DN_ALPHA * h_ref[...] + acc_ref[...], g_ref[...], b_ref[...])
        of_ref[...] = out
        ob_ref[...] = out.astype(bf16)


def _mlp(xb, wu, wd, h, g, b):
    row = lambda i, f: (i, 0)
    const = lambda i, f: (0, 0)
    return pl.pallas_call(
        _mlp_kernel,
        grid=(LP // TM_OUT, D_FF // TF),
        in_specs=[pl.BlockSpec((TM_OUT, D_MODEL), row),
                  pl.BlockSpec((D_MODEL, TF), lambda i, f: (0, f)),
                  pl.BlockSpec((TF, D_MODEL), lambda i, f: (f, 0)),
                  pl.BlockSpec((TM_OUT, D_MODEL), row),
                  pl.BlockSpec((1, D_MODEL), const),
                  pl.BlockSpec((1, D_MODEL), const)],
        out_specs=[pl.BlockSpec((TM_OUT, D_MODEL), row),
                   pl.BlockSpec((TM_OUT, D_MODEL), row)],
        out_shape=[jax.ShapeDtypeStruct((LP, D_MODEL), f32),
                   jax.ShapeDtypeStruct((LP, D_MODEL), bf16)],
        scratch_shapes=[pltpu.VMEM((TM_OUT, D_MODEL), f32)],
        compiler_params=pltpu.CompilerParams(
            dimension_semantics=("arbitrary", "arbitrary"), vmem_limit_bytes=VMEM_LIMIT),
        name="mlp",
    )(xb, wu, wd, h, g, b)


def _cast_kernel(x_ref, o_ref):
    o_ref[...] = x_ref[...].astype(o_ref.dtype)


def _to_bf16(w):
    rows, cols = w.shape
    tr = CAST_BLOCK_BYTES // (cols * 4)
    return pl.pallas_call(
        _cast_kernel,
        grid=(rows // tr,),
        in_specs=[pl.BlockSpec((tr, cols), lambda i: (i, 0))],
        out_specs=pl.BlockSpec((tr, cols), lambda i: (i, 0)),
        out_shape=jax.ShapeDtypeStruct((rows, cols), bf16),
        compiler_params=pltpu.CompilerParams(
            dimension_semantics=("arbitrary",), vmem_limit_bytes=VMEM_LIMIT),
        name="cast",
    )(w)


def _prep_w_in_kernel(w_ref, big_ref, bigt_ref):
    o_mo = 2 * M_QK_W + M_V_W + 2 * M_HEADS
    o_fq = o_mo + M_V_W
    o_fk = o_fq + F_W
    o_fv = o_fk + F_W
    n_lead = 2 * M_QK_W + M_V_W
    big_ref[:, :n_lead] = w_ref[:, :n_lead].astype(bf16)
    big_ref[:, n_lead:n_lead + M_V_W] = w_ref[:, o_mo:o_mo + M_V_W].astype(bf16)
    big_ref[:, n_lead + M_V_W:] = w_ref[:, o_fk:o_fk + F_W].astype(bf16)
    bigt_ref[0] = (w_ref[:, o_fq:o_fq + F_W] * (F_HD ** -0.5 * LOG2E)).T.astype(bf16)
    bigt_ref[1] = w_ref[:, o_fv:o_fv + F_W].T.astype(bf16)


def _prep_w_in(w):
    in_width = w.shape[1]
    return pl.pallas_call(
        _prep_w_in_kernel,
        grid=(D_MODEL // BLK,),
        in_specs=[pl.BlockSpec((BLK, in_width), lambda i: (i, 0))],
        out_specs=[pl.BlockSpec((BLK, N_GROUPS * GROUP_W), lambda i: (i, 0)),
                   pl.BlockSpec((N_GROUPS_T, GROUP_W, BLK), lambda i: (0, 0, i))],
        out_shape=[jax.ShapeDtypeStruct((D_MODEL, N_GROUPS * GROUP_W), bf16),
                   jax.ShapeDtypeStruct((N_GROUPS_T, GROUP_W, D_MODEL), bf16)],
        compiler_params=pltpu.CompilerParams(
            dimension_semantics=("arbitrary",), vmem_limit_bytes=VMEM_LIMIT),
        name="prep_w_in",
    )(w)


def _split_w_in(w):
    o = 2 * M_QK_W + M_V_W
    mi_mf = w[:, o:o + 2 * M_HEADS]
    ff = w[:, w.shape[1] - F_HEADS:]
    n_gate = 2 * M_HEADS + F_HEADS
    gate = jnp.concatenate([mi_mf, ff, jnp.zeros((D_MODEL, LANES - n_gate), f32)], axis=1)
    big, big_t = _prep_w_in(w)
    return big, big_t, gate


def kernel(x, meta, w_in, conv_w, conv_b, m_i_bias, m_f_bias, m_norm, f_f_bias, f_norm, w_out,
           ln1_g, ln1_b, w_up, w_down, ln2_g, ln2_b):
    assert x.shape == (1, SEQ, D_MODEL)
    h = jnp.concatenate([jnp.zeros((P_PAD, D_MODEL), f32), meta.astype(f32), x[0].astype(f32)], axis=0)
    hb = h.astype(bf16)
    n_gate = 2 * M_HEADS + F_HEADS
    for l in range(DEPTH):
        w_big, w_big_t, w_gate = _split_w_in(w_in[l])
        b_gate = jnp.concatenate([m_i_bias[l], m_f_bias[l], f_f_bias[l],
                                  jnp.zeros((LANES - n_gate,), f32)])[None, :]
        G, GT, kb, qbt = _gates(h, w_gate, b_gate)
        proj = _inproj(hb, w_big)
        proj_t = _inproj_t(hb, w_big_t)
        hm = _mlstm(proj, G, GT, conv_w[l], conv_b[l][None, :], m_norm[l][None, :])
        hf = _fox(proj, proj_t, kb, qbt, f_norm[l].reshape(F_HEADS, 1, F_HD))
        h1, h1b = _outproj(hm, hf, _to_bf16(w_out[l]), h, ln1_g[l][None, :], ln1_b[l][None, :])
        h, hb = _mlp(h1b, _to_bf16(w_up[l]), _to_bf16(w_down[l]), h1,
                     ln2_g[l][None, :], ln2_b[l][None, :])
    return h[P_PAD + N_META:][None]
```

```python
import functools

import jax
import jax.numpy as jnp
from jax import lax
from jax.experimental import pallas as pl
from jax.experimental.pallas import tpu as pltpu

D_MODEL = 2048
SEQ = 16384
DEPTH = 2
N_META = 16
M_HEADS = 4
M_QK = 128
M_V = 256
CONV_W = 4
F_HEADS = 8
F_HD = 128
D_FF = 4 * D_MODEL
M_QK_W = M_HEADS * M_QK
M_V_W = M_HEADS * M_V
F_W = F_HEADS * F_HD
DN_ALPHA = (2 * DEPTH) ** 0.25
LN_EPS = 1e-5
RMS_EPS = 1e-6
NEG = -1e30

LANES = 128
BLK = 256
P_PAD = BLK - N_META
LP = P_PAD + N_META + SEQ
NBLK = LP // BLK
GROUP_W = 1024
N_GROUPS = 4
N_GROUPS_T = 2
TQ = 1280
SUB = TQ // BLK
N_BIAS = 3
ONES_ROWS = 16
UNROLL = 2
LOOKAHEAD = 5
LOG2E = 1.4426950408889634
TM_PROJ = 1040
MLP_ROW_TILES = 32
TF = 1024
C_AUG = M_V + LANES
VMEM_LIMIT = 56 * 1024 * 1024
CAST_BLOCK_BYTES = 8 * 1024 * 1024

f32 = jnp.float32
bf16 = jnp.bfloat16


def _layer_norm(z, g, b):
    mu = jnp.mean(z, axis=-1, keepdims=True)
    zc = z - mu
    var = jnp.mean(zc * zc, axis=-1, keepdims=True)
    return zc * lax.rsqrt(var + LN_EPS) * g + b


def _log_sigmoid(x):
    return jnp.minimum(x, 0.0) - jnp.log(1.0 + jnp.exp(-jnp.abs(x)))


def _bf16_piece(x):
    return x.astype(bf16).astype(f32)


def _gates_kernel(first_ref, rest_ref, w_ref, b_ref, g_ref, gt_ref, kb_ref, qbt_ref,
                  carry_ref, wsplit_ref):
    i = pl.program_id(0)

    @pl.when(i == 0)
    def _():
        carry_ref[...] = jnp.zeros_like(carry_ref)
        w = w_ref[...]
        w_hi = _bf16_piece(w)
        wsplit_ref[:, :LANES] = w_hi.astype(bf16)
        wsplit_ref[:, LANES:] = (w - w_hi).astype(bf16)

    x = jnp.where(i == 0, first_ref[...], rest_ref[...])
    x_hi = _bf16_piece(x)
    both = jnp.dot(x_hi.astype(bf16), wsplit_ref[...], preferred_element_type=f32)
    low = jnp.dot((x - x_hi).astype(bf16), wsplit_ref[:, :LANES], preferred_element_type=f32)
    pre = both[:, :LANES] + (both[:, LANES:] + low) + b_ref[...]
    row = i * BLK + lax.broadcasted_iota(jnp.int32, (BLK, LANES), 0)
    col = lax.broadcasted_iota(jnp.int32, (BLK, LANES), 1)
    valid = row >= P_PAD
    is_f = (col >= M_HEADS) & (col < 2 * M_HEADS + F_HEADS)
    lf = jnp.where(valid & is_f, _log_sigmoid(pre), 0.0)
    r2 = lax.broadcasted_iota(jnp.int32, (BLK, BLK), 0)
    c2 = lax.broadcasted_iota(jnp.int32, (BLK, BLK), 1)
    tri = jnp.where(c2 <= r2, 1.0, 0.0).astype(f32)
    cum = jnp.dot(tri, lf, preferred_element_type=f32,
                  precision=lax.Precision.HIGHEST) + carry_ref[...]
    carry_ref[...] = cum[BLK - 1:BLK, :]
    out = jnp.where(col < M_HEADS, jnp.where(valid, pre, NEG), cum)
    g_ref[...] = out
    gt_ref[...] = out.T

    valid_col = i * BLK + lax.broadcasted_iota(jnp.int32, (BLK, 1), 0) >= P_PAD
    for h in range(F_HEADS):
        c = cum[:, 2 * M_HEADS + h:2 * M_HEADS + h + 1] * LOG2E
        pieces = []
        rest = c
        for _ in range(N_BIAS):
            piece = _bf16_piece(rest)
            pieces.append(piece)
            rest = rest - piece
        tile_k = jnp.where(col < N_BIAS, 1.0, 0.0)
        tile_q = jnp.where((col >= N_BIAS) & (col < 2 * N_BIAS), 1.0, 0.0)
        for n, piece in enumerate(pieces):
            key_piece = jnp.where(valid_col, -piece, NEG) if n == 0 else -piece
            tile_k = jnp.where(col == N_BIAS + n, key_piece, tile_k)
            tile_q = jnp.where(col == n, piece, tile_q)
        kb_ref[h] = tile_k.astype(bf16)
        qbt_ref[h] = tile_q.T.astype(bf16)


def _gates(h, head, x, wg, bg):
    first, rest, rest_off = _stream_sources(h, head, x)
    return pl.pallas_call(
        _gates_kernel,
        grid=(NBLK,),
        in_specs=_stream_specs(rest_off, 0)
                 + [pl.BlockSpec((D_MODEL, LANES), lambda i: (0, 0)),
                    pl.BlockSpec((1, LANES), lambda i: (0, 0))],
        out_specs=[pl.BlockSpec((BLK, LANES), lambda i: (i, 0)),
                   pl.BlockSpec((LANES, BLK), lambda i: (0, i)),
                   pl.BlockSpec((F_HEADS, BLK, LANES), lambda i: (0, i, 0)),
                   pl.BlockSpec((F_HEADS, LANES, BLK), lambda i: (0, 0, i))],
        out_shape=[jax.ShapeDtypeStruct((LP, LANES), f32),
                   jax.ShapeDtypeStruct((LANES, LP), f32),
                   jax.ShapeDtypeStruct((F_HEADS, LP, LANES), bf16),
                   jax.ShapeDtypeStruct((F_HEADS, LANES, LP), bf16)],
        scratch_shapes=[pltpu.VMEM((1, LANES), f32),
                        pltpu.VMEM((D_MODEL, 2 * LANES), bf16)],
        compiler_params=pltpu.CompilerParams(dimension_semantics=("arbitrary",)),
        name="gates",
    )(first, rest, wg, bg)


def _inproj_kernel(x_ref, w_ref, o_ref):
    o_ref[...] = jnp.dot(x_ref[...], w_ref[...], preferred_element_type=f32).astype(o_ref.dtype)


def _inproj(xb, w):
    return pl.pallas_call(
        _inproj_kernel,
        grid=(N_GROUPS, LP // TM_PROJ),
        in_specs=[pl.BlockSpec((TM_PROJ, D_MODEL), lambda j, i: (i, 0)),
                  pl.BlockSpec((D_MODEL, GROUP_W), lambda j, i: (0, j))],
        out_specs=pl.BlockSpec((None, TM_PROJ, GROUP_W), lambda j, i: (j, i, 0)),
        out_shape=jax.ShapeDtypeStruct((N_GROUPS, LP, GROUP_W), bf16),
        compiler_params=pltpu.CompilerParams(
            dimension_semantics=("arbitrary", "arbitrary"), vmem_limit_bytes=VMEM_LIMIT),
        name="inproj",
    )(xb, w)


def _inproj_t_kernel(x_ref, wt_ref, o_ref):
    o_ref[...] = lax.dot_general(wt_ref[...], x_ref[...], (((1,), (1,)), ((), ())),
                                 preferred_element_type=f32).astype(o_ref.dtype)


def _inproj_t(xb, wt):
    return pl.pallas_call(
        _inproj_t_kernel,
        grid=(N_GROUPS_T, LP // TQ),
        in_specs=[pl.BlockSpec((TQ, D_MODEL), lambda j, i: (i, 0)),
                  pl.BlockSpec((None, GROUP_W, D_MODEL), lambda j, i: (j, 0, 0))],
        out_specs=pl.BlockSpec((None, GROUP_W, TQ), lambda j, i: (j, 0, i)),
        out_shape=jax.ShapeDtypeStruct((N_GROUPS_T, GROUP_W, LP), bf16),
        compiler_params=pltpu.CompilerParams(
            dimension_semantics=("arbitrary", "arbitrary"), vmem_limit_bytes=VMEM_LIMIT),
        name="inproj_t",
    )(xb, wt)


def _mlstm_kernel(qk_ref, v_ref, og_ref, g_ref, gt_ref, cw_ref, cb_ref, nrm_ref, out_ref,
                  c_scr, m_scr, cf_scr, conv_scr):
    i = pl.program_id(0)
    T = BLK

    @pl.when(i == 0)
    def _():
        c_scr[...] = jnp.zeros_like(c_scr)
        m_scr[...] = jnp.zeros_like(m_scr)
        cf_scr[...] = jnp.zeros_like(cf_scr)
        conv_scr[...] = jnp.zeros_like(conv_scr)

    row = i * T + lax.broadcasted_iota(jnp.int32, (T, 1), 0)
    x = jnp.where(row >= P_PAD, qk_ref[...].astype(f32), 0.0)
    xs = jnp.concatenate([conv_scr[...], x], axis=0)
    cw = cw_ref[...]
    conv = cb_ref[...]
    for k in range(CONV_W):
        off = 8 - (CONV_W - 1) + k
        conv = conv + cw[k:k + 1, :] * xs[off:off + T, :]
    conv_scr[...] = x[T - 8:T, :]
    act = conv * jax.nn.sigmoid(conv)

    G = g_ref[...]
    GT = gt_ref[...]
    cfs_all = cf_scr[...]
    cf_scr[...] = G[T - 1:T, :]
    r2 = lax.broadcasted_iota(jnp.int32, (T, T), 0)
    c2 = lax.broadcasted_iota(jnp.int32, (T, T), 1)
    causal = c2 <= r2
    ones_blk = jnp.where(lax.broadcasted_iota(jnp.int32, (T, LANES), 1) == 0, 1.0, 0.0).astype(bf16)

    for h in range(M_HEADS):
        qh = (act[:, h * M_QK:(h + 1) * M_QK] * (M_QK ** -0.5)).astype(bf16)
        kf = act[:, M_QK_W + h * M_QK:M_QK_W + (h + 1) * M_QK]
        kh = kf.astype(bf16)
        v_aug = jnp.concatenate([v_ref[:, h * M_V:(h + 1) * M_V], ones_blk], axis=1)
        cfs = cfs_all[:, M_HEADS + h:M_HEADS + h + 1]
        b_col = G[:, M_HEADS + h:M_HEADS + h + 1] - cfs
        b_row = GT[M_HEADS + h:M_HEADS + h + 1, :] - cfs
        li_col = G[:, h:h + 1]
        li_row = GT[h:h + 1, :]
        g_tot = b_col[T - 1:T, :]
        m_prev = m_scr[h][:, 0:1]
        c_prev = c_scr[h]

        d = jnp.where(causal, b_col - b_row + li_row, NEG)
        inter_log = b_col + m_prev
        m_t = jnp.maximum(inter_log, jnp.max(d, axis=-1, keepdims=True))
        s = lax.dot_general(qh, kh, (((1,), (1,)), ((), ())), preferred_element_type=f32)
        w = (jnp.exp(d - m_t) * s).astype(bf16)
        s_inter = jnp.exp(inter_log - m_t)
        num = jnp.dot(w, v_aug, preferred_element_type=f32) + s_inter * jnp.dot(
            qh, c_prev.astype(bf16), preferred_element_type=f32)
        nq = num[:, M_V:M_V + 1]
        hv = num[:, :M_V] / jnp.maximum(jnp.abs(nq), jnp.exp(-m_t))
        hv = hv * lax.rsqrt(jnp.mean(hv * hv, axis=-1, keepdims=True) + RMS_EPS)
        gate = jax.nn.sigmoid(og_ref[:, h * M_V:(h + 1) * M_V].astype(f32))
        out_ref[:, h * M_V:(h + 1) * M_V] = (
            hv * nrm_ref[:, h * M_V:(h + 1) * M_V] * gate).astype(out_ref.dtype)

        a_col = g_tot - b_col + li_col
        m_loc = jnp.max(a_col, axis=0, keepdims=True)
        kw_t = (kf * jnp.exp(a_col - m_loc)).T.astype(bf16)
        c_loc = jnp.dot(kw_t, v_aug, preferred_element_type=f32)
        m_new = jnp.maximum(g_tot + m_prev, m_loc)
        c_scr[h] = jnp.exp(g_tot + m_prev - m_new) * c_prev + jnp.exp(m_loc - m_new) * c_loc
        m_scr[h] = jnp.broadcast_to(m_new, (1, LANES))


def _mlstm(proj, G, GT, conv_w, conv_b, m_norm):
    return pl.pallas_call(
        _mlstm_kernel,
        grid=(NBLK,),
        in_specs=[pl.BlockSpec((None, BLK, GROUP_W), lambda i: (0, i, 0)),
                  pl.BlockSpec((None, BLK, GROUP_W), lambda i: (1, i, 0)),
                  pl.BlockSpec((None, BLK, GROUP_W), lambda i: (2, i, 0)),
                  pl.BlockSpec((BLK, LANES), lambda i: (i, 0)),
                  pl.BlockSpec((8, BLK), lambda i: (0, i)),
                  pl.BlockSpec((CONV_W, 2 * M_QK_W), lambda i: (0, 0)),
                  pl.BlockSpec((1, 2 * M_QK_W), lambda i: (0, 0)),
                  pl.BlockSpec((1, M_V_W), lambda i: (0, 0))],
        out_specs=pl.BlockSpec((BLK, M_V_W), lambda i: (i, 0)),
        out_shape=jax.ShapeDtypeStruct((LP, M_V_W), bf16),
        scratch_shapes=[pltpu.VMEM((M_HEADS, M_QK, C_AUG), f32),
                        pltpu.VMEM((M_HEADS, 1, LANES), f32),
                        pltpu.VMEM((1, LANES), f32),
                        pltpu.VMEM((8, 2 * M_QK_W), f32)],
        compiler_params=pltpu.CompilerParams(
            dimension_semantics=("arbitrary",), vmem_limit_bytes=VMEM_LIMIT),
        name="mlstm",
    )(proj, proj, proj, G, GT, conv_w, conv_b, m_norm)


def _fox_kernel(qt_ref, qbt_ref, k_ref, kb_ref, vt_ref, nrm_ref, o_ref, m_scr, acc_scr):
    qi = pl.program_id(1)
    q_aug = jnp.concatenate([qt_ref[...], qbt_ref[...]], axis=0)
    m_scr[...] = jnp.full_like(m_scr, NEG)
    acc_scr[...] = jnp.zeros_like(acc_scr)

    ones_rows = jnp.ones((ONES_ROWS, BLK), bf16)

    def scores(c, start, masked):
        cols = slice(c * BLK, (c + 1) * BLK)
        k_aug = jnp.concatenate([k_ref[pl.ds(start, BLK), :], kb_ref[pl.ds(start, BLK), :]], axis=1)
        s = jnp.dot(k_aug, q_aug[:, cols], preferred_element_type=f32)
        if masked:
            kpos = start + lax.broadcasted_iota(jnp.int32, (BLK, 1), 0)
            qpos = qi * TQ + c * BLK + lax.broadcasted_iota(jnp.int32, (1, BLK), 1)
            s = jnp.where(kpos <= qpos, s, NEG)
        m_prev = m_scr[:, cols]
        m_new = jnp.maximum(m_prev, jnp.max(s, axis=0, keepdims=True))
        m_scr[:, cols] = m_new
        return s, m_prev, m_new

    def accumulate(c, start, s, m_prev, m_new):
        cols = slice(c * BLK, (c + 1) * BLK)
        v_aug = jnp.concatenate([vt_ref[:, pl.ds(start, BLK)], ones_rows], axis=0)
        p = jnp.exp2(s - m_new).astype(bf16)
        alpha = jnp.exp2(m_prev - m_new)
        acc_scr[:, cols] = alpha * acc_scr[:, cols] + jnp.dot(v_aug, p, preferred_element_type=f32)

    def run(units):
        pending = []
        for c, start, masked in units:
            pending.append((c, start) + scores(c, start, masked))
            if len(pending) > LOOKAHEAD:
                accumulate(*pending.pop(0))
        for unit in pending:
            accumulate(*unit)

    def full_blocks(first_key, n_blocks):
        run([(c, pl.multiple_of(first_key + d * BLK, BLK), False)
             for d in range(n_blocks) for c in range(SUB)])

    def body(j, carry):
        full_blocks(pl.multiple_of(j * (UNROLL * TQ), TQ), UNROLL * SUB)
        return carry

    lax.fori_loop(0, qi // UNROLL, body, 0)
    for r in range(1, UNROLL):
        @pl.when(qi % UNROLL >= r)
        def _():
            full_blocks(pl.multiple_of((qi // UNROLL * UNROLL + r - 1) * TQ, TQ), SUB)

    tail = pl.multiple_of(qi * TQ, TQ)
    run([(c, pl.multiple_of(tail + d * BLK, BLK), c == d) for d in range(SUB) for c in range(d, SUB)])

    acc = acc_scr[...]
    ot = acc[:F_HD, :] / acc[F_HD:F_HD + 1, :]
    ot = ot * lax.rsqrt(jnp.mean(ot * ot, axis=0, keepdims=True) + RMS_EPS)
    o_ref[...] = (ot.T * nrm_ref[...]).astype(o_ref.dtype)


def _fox(proj, proj_t, kb, qbt, f_norm):
    return pl.pallas_call(
        _fox_kernel,
        grid=(F_HEADS, LP // TQ),
        in_specs=[pl.BlockSpec((None, F_HD, TQ), lambda h, i: (0, h, i)),
                  pl.BlockSpec((None, LANES, TQ), lambda h, i: (h, 0, i)),
                  pl.BlockSpec((None, LP, F_HD), lambda h, i: (3, 0, h)),
                  pl.BlockSpec((None, LP, LANES), lambda h, i: (h, 0, 0)),
                  pl.BlockSpec((None, F_HD, LP), lambda h, i: (1, h, 0)),
                  pl.BlockSpec((None, 1, F_HD), lambda h, i: (h, 0, 0))],
        out_specs=pl.BlockSpec((TQ, F_HD), lambda h, i: (i, h)),
        out_shape=jax.ShapeDtypeStruct((LP, F_W), bf16),
        scratch_shapes=[pltpu.VMEM((1, TQ), f32),
                        pltpu.VMEM((F_HD + ONES_ROWS, TQ), f32)],
        compiler_params=pltpu.CompilerParams(
            dimension_semantics=("arbitrary", "arbitrary"), vmem_limit_bytes=VMEM_LIMIT),
        name="fox",
    )(proj_t, qbt, proj, kb, proj_t, f_norm)


def _stream_sources(h, head, x):
    if h is None:
        return head, x.reshape(SEQ, D_MODEL), 0
    return h, h, 1


def _stream_specs(rest_off, skip):
    return [pl.BlockSpec((BLK, D_MODEL), lambda i: (0, 0)),
            pl.BlockSpec((BLK, D_MODEL), lambda i: (jnp.maximum(i + skip - 1, 0) + rest_off, 0))]


def _outproj_kernel(hm_ref, hf_ref, w_ref, first_ref, rest_ref, g_ref, b_ref, of_ref, ob_ref, *, skip):
    y = jnp.dot(hm_ref[...], w_ref[:M_V_W, :], preferred_element_type=f32)
    y = y + jnp.dot(hf_ref[...], w_ref[M_V_W:, :], preferred_element_type=f32)
    res = rest_ref[...]
    if skip == 0:
        res = jnp.where(pl.program_id(0) == 0, first_ref[...], res)
    out = _layer_norm(DN_ALPHA * res + y, g_ref[...], b_ref[...])
    of_ref[...] = out
    ob_ref[...] = out.astype(bf16)


def _outproj(hm, hf, w, h, head, x, g, b, real_rows_only):
    skip = 1 if real_rows_only else 0
    first, rest, rest_off = _stream_sources(h, head, x)
    rows = LP - skip * BLK
    shifted = lambda i: (i + skip, 0)
    row = lambda i: (i, 0)
    const = lambda i: (0, 0)
    return pl.pallas_call(
        functools.partial(_outproj_kernel, skip=skip),
        grid=(rows // BLK,),
        in_specs=[pl.BlockSpec((BLK, M_V_W), shifted),
                  pl.BlockSpec((BLK, F_W), shifted),
                  pl.BlockSpec((M_V_W + F_W, D_MODEL), const)]
                 + _stream_specs(rest_off, skip)
                 + [pl.BlockSpec((1, D_MODEL), const),
                    pl.BlockSpec((1, D_MODEL), const)],
        out_specs=[pl.BlockSpec((BLK, D_MODEL), row),
                   pl.BlockSpec((BLK, D_MODEL), row)],
        out_shape=[jax.ShapeDtypeStruct((rows, D_MODEL), f32),
                   jax.ShapeDtypeStruct((rows, D_MODEL), bf16)],
        compiler_params=pltpu.CompilerParams(
            dimension_semantics=("arbitrary",), vmem_limit_bytes=VMEM_LIMIT),
        name="outproj",
    )(hm, hf, w, first, rest, g, b)


def _mlp_kernel(xb_ref, wu_ref, wd_ref, h_ref, g_ref, b_ref, of_ref, ob_ref, acc_ref):
    f = pl.program_id(1)

    @pl.when(f == 0)
    def _():
        acc_ref[...] = jnp.zeros_like(acc_ref)

    u = jnp.maximum(jnp.dot(xb_ref[...], wu_ref[...], preferred_element_type=f32), 0.0)
    acc_ref[...] += jnp.dot((u * u).astype(bf16), wd_ref[...], preferred_element_type=f32)

    @pl.when(f == pl.num_programs(1) - 1)
    def _():
        out = _layer_norm(DN_ALPHA * h_ref[...] + acc_ref[...], g_ref[...], b_ref[...])
        of_ref[...] = out
        ob_ref[...] = out.astype(bf16)


def _mlp(xb, wu, wd, h, g, b):
    rows = xb.shape[0]
    tm = rows // MLP_ROW_TILES
    assert tm * MLP_ROW_TILES == rows and tm % 8 == 0
    row = lambda i, f: (i, 0)
    const = lambda i, f: (0, 0)
    return pl.pallas_call(
        _mlp_kernel,
        grid=(MLP_ROW_TILES, D_FF // TF),
        in_specs=[pl.BlockSpec((tm, D_MODEL), row),
                  pl.BlockSpec((D_MODEL, TF), lambda i, f: (0, f)),
                  pl.BlockSpec((TF, D_MODEL), lambda i, f: (f, 0)),
                  pl.BlockSpec((tm, D_MODEL), row),
                  pl.BlockSpec((1, D_MODEL), const),
                  pl.BlockSpec((1, D_MODEL), const)],
        out_specs=[pl.BlockSpec((tm, D_MODEL), row),
                   pl.BlockSpec((tm, D_MODEL), row)],
        out_shape=[jax.ShapeDtypeStruct((rows, D_MODEL), f32),
                   jax.ShapeDtypeStruct((rows, D_MODEL), bf16)],
        scratch_shapes=[pltpu.VMEM((tm, D_MODEL), f32)],
        compiler_params=pltpu.CompilerParams(
            dimension_semantics=("arbitrary", "arbitrary"), vmem_limit_bytes=VMEM_LIMIT),
        name="mlp",
    )(xb, wu, wd, h, g, b)


def _cast_kernel(x_ref, o_ref):
    o_ref[...] = x_ref[...].astype(o_ref.dtype)


def _to_bf16(w, l):
    _, rows, cols = w.shape
    tr = CAST_BLOCK_BYTES // (cols * 4)
    return pl.pallas_call(
        _cast_kernel,
        grid=(rows // tr,),
        in_specs=[pl.BlockSpec((None, tr, cols), lambda i: (l, i, 0))],
        out_specs=pl.BlockSpec((tr, cols), lambda i: (i, 0)),
        out_shape=jax.ShapeDtypeStruct((rows, cols), bf16),
        compiler_params=pltpu.CompilerParams(
            dimension_semantics=("arbitrary",), vmem_limit_bytes=VMEM_LIMIT),
        name="cast",
    )(w)


def _prep_w_in_kernel(w_ref, big_ref, bigt_ref, gate_ref):
    o_gate = 2 * M_QK_W + M_V_W
    o_mo = o_gate + 2 * M_HEADS
    o_fq = o_mo + M_V_W
    o_fk = o_fq + F_W
    o_fv = o_fk + F_W
    o_ff = o_fv + F_W
    assert o_gate % LANES == 0 and o_ff % LANES == 2 * M_HEADS
    gate_ref[...] = jnp.zeros_like(gate_ref)
    gate_ref[:, :2 * M_HEADS] = w_ref[:, o_gate:o_gate + 2 * M_HEADS]
    gate_ref[:, 2 * M_HEADS:2 * M_HEADS + F_HEADS] = w_ref[:, o_ff:o_ff + F_HEADS]
    n_lead = 2 * M_QK_W + M_V_W
    big_ref[:, :n_lead] = w_ref[:, :n_lead].astype(bf16)
    big_ref[:, n_lead:n_lead + M_V_W] = w_ref[:, o_mo:o_mo + M_V_W].astype(bf16)
    big_ref[:, n_lead + M_V_W:] = w_ref[:, o_fk:o_fk + F_W].astype(bf16)
    bigt_ref[0] = (w_ref[:, o_fq:o_fq + F_W] * (F_HD ** -0.5 * LOG2E)).T.astype(bf16)
    bigt_ref[1] = w_ref[:, o_fv:o_fv + F_W].T.astype(bf16)


def _prep_w_in(w_in, l):
    in_width = w_in.shape[2]
    return pl.pallas_call(
        _prep_w_in_kernel,
        grid=(D_MODEL // BLK,),
        in_specs=[pl.BlockSpec((None, BLK, in_width), lambda i: (l, i, 0))],
        out_specs=[pl.BlockSpec((BLK, N_GROUPS * GROUP_W), lambda i: (i, 0)),
                   pl.BlockSpec((N_GROUPS_T, GROUP_W, BLK), lambda i: (0, 0, i)),
                   pl.BlockSpec((BLK, LANES), lambda i: (i, 0))],
        out_shape=[jax.ShapeDtypeStruct((D_MODEL, N_GROUPS * GROUP_W), bf16),
                   jax.ShapeDtypeStruct((N_GROUPS_T, GROUP_W, D_MODEL), bf16),
                   jax.ShapeDtypeStruct((D_MODEL, LANES), f32)],
        compiler_params=pltpu.CompilerParams(
            dimension_semantics=("arbitrary",), vmem_limit_bytes=VMEM_LIMIT),
        name="prep_w_in",
    )(w_in)


def kernel(x, meta, w_in, conv_w, conv_b, m_i_bias, m_f_bias, m_norm, f_f_bias, f_norm, w_out,
           ln1_g, ln1_b, w_up, w_down, ln2_g, ln2_b):
    assert x.shape == (1, SEQ, D_MODEL)
    head = jnp.concatenate([jnp.zeros((P_PAD, D_MODEL), f32), meta.astype(f32)], axis=0)
    hb = jnp.concatenate([head.astype(bf16), x[0].astype(bf16)], axis=0)
    h = None
    n_gate = 2 * M_HEADS + F_HEADS
    for l in range(DEPTH):
        last = l == DEPTH - 1
        w_big, w_big_t, w_gate = _prep_w_in(w_in, l)
        b_gate = jnp.concatenate([m_i_bias[l], m_f_bias[l], f_f_bias[l],
                                  jnp.zeros((LANES - n_gate,), f32)])[None, :]
        G, GT, kb, qbt = _gates(h, head, x, w_gate, b_gate)
        proj = _inproj(hb, w_big)
        proj_t = _inproj_t(hb, w_big_t)
        hm = _mlstm(proj, G, GT, conv_w[l], conv_b[l][None, :], m_norm[l][None, :])
        hf = _fox(proj, proj_t, kb, qbt, f_norm[l].reshape(F_HEADS, 1, F_HD))
        h1, h1b = _outproj(hm, hf, _to_bf16(w_out, l), h, head, x,
                           ln1_g[l][None, :], ln1_b[l][None, :], real_rows_only=last)
        h, hb = _mlp(h1b, _to_bf16(w_up, l), _to_bf16(w_down, l), h1,
                     ln2_g[l][None, :], ln2_b[l][None, :])
    return h[None]
```

```python
import functools

import jax
import jax.numpy as jnp
from jax import lax
from jax.experimental import pallas as pl
from jax.experimental.pallas import tpu as pltpu

D_MODEL = 2048
SEQ = 16384
DEPTH = 2
N_META = 16
M_HEADS = 4
M_QK = 128
M_V = 256
CONV_W = 4
F_HEADS = 8
F_HD = 128
D_FF = 4 * D_MODEL
M_QK_W = M_HEADS * M_QK
M_V_W = M_HEADS * M_V
F_W = F_HEADS * F_HD
DN_ALPHA = (2 * DEPTH) ** 0.25
LN_EPS = 1e-5
RMS_EPS = 1e-6
NEG = -1e30

LANES = 128
BLK = 256
P_PAD = BLK - N_META
LP = P_PAD + N_META + SEQ
NBLK = LP // BLK
GROUP_W = 1024
N_GROUPS = 4
N_GROUPS_T = 2
TQ = 1280
SUB = TQ // BLK
N_BIAS = 3
ONES_ROWS = 16
UNROLL = 2
LOOKAHEAD = 5
LOG2E = 1.4426950408889634
TM_PROJ = 1040
MLP_ROW_TILES = 32
TF = 1024
C_AUG = M_V + LANES
VMEM_LIMIT = 56 * 1024 * 1024
CAST_BLOCK_BYTES = 8 * 1024 * 1024

f32 = jnp.float32
bf16 = jnp.bfloat16


def _layer_norm(z, g, b):
    mu = jnp.mean(z, axis=-1, keepdims=True)
    zc = z - mu
    var = jnp.mean(zc * zc, axis=-1, keepdims=True)
    return zc * lax.rsqrt(var + LN_EPS) * g + b


def _log_sigmoid(x):
    return jnp.minimum(x, 0.0) - jnp.log(1.0 + jnp.exp(-jnp.abs(x)))


def _bf16_piece(x):
    return x.astype(bf16).astype(f32)


def _gates_kernel(first_ref, rest_ref, w_ref, b_ref, g_ref, gt_ref, kb_ref, qbt_ref,
                  carry_ref, wsplit_ref):
    i = pl.program_id(0)

    @pl.when(i == 0)
    def _():
        carry_ref[...] = jnp.zeros_like(carry_ref)
        w = w_ref[...]
        w_hi = _bf16_piece(w)
        wsplit_ref[:, :LANES] = w_hi.astype(bf16)
        wsplit_ref[:, LANES:] = (w - w_hi).astype(bf16)

    x = jnp.where(i == 0, first_ref[...], rest_ref[...])
    x_hi = _bf16_piece(x)
    both = jnp.dot(x_hi.astype(bf16), wsplit_ref[...], preferred_element_type=f32)
    low = jnp.dot((x - x_hi).astype(bf16), wsplit_ref[:, :LANES], preferred_element_type=f32)
    pre = both[:, :LANES] + (both[:, LANES:] + low) + b_ref[...]
    row = i * BLK + lax.broadcasted_iota(jnp.int32, (BLK, LANES), 0)
    col = lax.broadcasted_iota(jnp.int32, (BLK, LANES), 1)
    valid = row >= P_PAD
    is_f = (col >= M_HEADS) & (col < 2 * M_HEADS + F_HEADS)
    lf = jnp.where(valid & is_f, _log_sigmoid(pre), 0.0)
    r2 = lax.broadcasted_iota(jnp.int32, (BLK, BLK), 0)
    c2 = lax.broadcasted_iota(jnp.int32, (BLK, BLK), 1)
    tri = jnp.where(c2 <= r2, 1.0, 0.0).astype(f32)
    cum = jnp.dot(tri, lf, preferred_element_type=f32,
                  precision=lax.Precision.HIGHEST) + carry_ref[...]
    carry_ref[...] = cum[BLK - 1:BLK, :]
    out = jnp.where(col < M_HEADS, jnp.where(valid, pre, NEG), cum)
    g_ref[...] = out
    gt_ref[...] = out.T

    valid_col = i * BLK + lax.broadcasted_iota(jnp.int32, (BLK, 1), 0) >= P_PAD
    for h in range(F_HEADS):
        c = cum[:, 2 * M_HEADS + h:2 * M_HEADS + h + 1] * LOG2E
        pieces = []
        rest = c
        for _ in range(N_BIAS):
            piece = _bf16_piece(rest)
            pieces.append(piece)
            rest = rest - piece
        tile_k = jnp.where(col < N_BIAS, 1.0, 0.0)
        tile_q = jnp.where((col >= N_BIAS) & (col < 2 * N_BIAS), 1.0, 0.0)
        for n, piece in enumerate(pieces):
            key_piece = jnp.where(valid_col, -piece, NEG) if n == 0 else -piece
            tile_k = jnp.where(col == N_BIAS + n, key_piece, tile_k)
            tile_q = jnp.where(col == n, piece, tile_q)
        kb_ref[h] = tile_k.astype(bf16)
        qbt_ref[h] = tile_q.T.astype(bf16)


def _gates(h, head, x, wg, bg):
    first, rest, rest_off = _stream_sources(h, head, x)
    return pl.pallas_call(
        _gates_kernel,
        grid=(NBLK,),
        in_specs=_stream_specs(rest, rest_off, 0)
                 + [pl.BlockSpec((D_MODEL, LANES), lambda i: (0, 0)),
                    pl.BlockSpec((1, LANES), lambda i: (0, 0))],
        out_specs=[pl.BlockSpec((BLK, LANES), lambda i: (i, 0)),
                   pl.BlockSpec((LANES, BLK), lambda i: (0, i)),
                   pl.BlockSpec((F_HEADS, BLK, LANES), lambda i: (0, i, 0)),
                   pl.BlockSpec((F_HEADS, LANES, BLK), lambda i: (0, 0, i))],
        out_shape=[jax.ShapeDtypeStruct((LP, LANES), f32),
                   jax.ShapeDtypeStruct((LANES, LP), f32),
                   jax.ShapeDtypeStruct((F_HEADS, LP, LANES), bf16),
                   jax.ShapeDtypeStruct((F_HEADS, LANES, LP), bf16)],
        scratch_shapes=[pltpu.VMEM((1, LANES), f32),
                        pltpu.VMEM((D_MODEL, 2 * LANES), bf16)],
        compiler_params=pltpu.CompilerParams(dimension_semantics=("arbitrary",)),
        name="gates",
    )(first, rest, wg, bg)


def _inproj_kernel(x_ref, w_ref, o_ref):
    o_ref[...] = jnp.dot(x_ref[...], w_ref[...], preferred_element_type=f32).astype(o_ref.dtype)


def _inproj(xb, w):
    return pl.pallas_call(
        _inproj_kernel,
        grid=(N_GROUPS, LP // TM_PROJ),
        in_specs=[pl.BlockSpec((TM_PROJ, D_MODEL), lambda j, i: (i, 0)),
                  pl.BlockSpec((D_MODEL, GROUP_W), lambda j, i: (0, j))],
        out_specs=pl.BlockSpec((None, TM_PROJ, GROUP_W), lambda j, i: (j, i, 0)),
        out_shape=jax.ShapeDtypeStruct((N_GROUPS, LP, GROUP_W), bf16),
        compiler_params=pltpu.CompilerParams(
            dimension_semantics=("arbitrary", "arbitrary"), vmem_limit_bytes=VMEM_LIMIT),
        name="inproj",
    )(xb, w)


def _inproj_t_kernel(x_ref, wt_ref, o_ref):
    o_ref[...] = lax.dot_general(wt_ref[...], x_ref[...], (((1,), (1,)), ((), ())),
                                 preferred_element_type=f32).astype(o_ref.dtype)


def _inproj_t(xb, wt):
    return pl.pallas_call(
        _inproj_t_kernel,
        grid=(N_GROUPS_T, LP // TQ),
        in_specs=[pl.BlockSpec((TQ, D_MODEL), lambda j, i: (i, 0)),
                  pl.BlockSpec((None, GROUP_W, D_MODEL), lambda j, i: (j, 0, 0))],
        out_specs=pl.BlockSpec((None, GROUP_W, TQ), lambda j, i: (j, 0, i)),
        out_shape=jax.ShapeDtypeStruct((N_GROUPS_T, GROUP_W, LP), bf16),
        compiler_params=pltpu.CompilerParams(
            dimension_semantics=("arbitrary", "arbitrary"), vmem_limit_bytes=VMEM_LIMIT),
        name="inproj_t",
    )(xb, wt)


def _mlstm_kernel(qk_ref, v_ref, og_ref, g_ref, gt_ref, cw_ref, cb_ref, nrm_ref, out_ref,
                  c_scr, m_scr, cf_scr, conv_scr):
    i = pl.program_id(0)
    T = BLK

    @pl.when(i == 0)
    def _():
        c_scr[...] = jnp.zeros_like(c_scr)
        m_scr[...] = jnp.zeros_like(m_scr)
        cf_scr[...] = jnp.zeros_like(cf_scr)
        conv_scr[...] = jnp.zeros_like(conv_scr)

    row = i * T + lax.broadcasted_iota(jnp.int32, (T, 1), 0)
    x = jnp.where(row >= P_PAD, qk_ref[...].astype(f32), 0.0)
    xs = jnp.concatenate([conv_scr[...], x], axis=0)
    cw = cw_ref[...]
    conv = cb_ref[...]
    for k in range(CONV_W):
        off = 8 - (CONV_W - 1) + k
        conv = conv + cw[k:k + 1, :] * xs[off:off + T, :]
    conv_scr[...] = x[T - 8:T, :]
    act = conv * jax.nn.sigmoid(conv)

    G = g_ref[...]
    GT = gt_ref[...]
    cfs_all = cf_scr[...]
    cf_scr[...] = G[T - 1:T, :]
    r2 = lax.broadcasted_iota(jnp.int32, (T, T), 0)
    c2 = lax.broadcasted_iota(jnp.int32, (T, T), 1)
    causal = c2 <= r2
    ones_blk = jnp.where(lax.broadcasted_iota(jnp.int32, (T, LANES), 1) == 0, 1.0, 0.0).astype(bf16)

    for h in range(M_HEADS):
        qh = (act[:, h * M_QK:(h + 1) * M_QK] * (M_QK ** -0.5)).astype(bf16)
        kf = act[:, M_QK_W + h * M_QK:M_QK_W + (h + 1) * M_QK]
        kh = kf.astype(bf16)
        v_aug = jnp.concatenate([v_ref[:, h * M_V:(h + 1) * M_V], ones_blk], axis=1)
        cfs = cfs_all[:, M_HEADS + h:M_HEADS + h + 1]
        b_col = G[:, M_HEADS + h:M_HEADS + h + 1] - cfs
        b_row = GT[M_HEADS + h:M_HEADS + h + 1, :] - cfs
        li_col = G[:, h:h + 1]
        li_row = GT[h:h + 1, :]
        g_tot = b_col[T - 1:T, :]
        m_prev = m_scr[h][:, 0:1]
        c_prev = c_scr[h]

        d = jnp.where(causal, b_col - b_row + li_row, NEG)
        inter_log = b_col + m_prev
        m_t = jnp.maximum(inter_log, jnp.max(d, axis=-1, keepdims=True))
        s = lax.dot_general(qh, kh, (((1,), (1,)), ((), ())), preferred_element_type=f32)
        w = (jnp.exp(d - m_t) * s).astype(bf16)
        s_inter = jnp.exp(inter_log - m_t)
        num = jnp.dot(w, v_aug, preferred_element_type=f32) + s_inter * jnp.dot(
            qh, c_prev.astype(bf16), preferred_element_type=f32)
        nq = num[:, M_V:M_V + 1]
        hv = num[:, :M_V] / jnp.maximum(jnp.abs(nq), jnp.exp(-m_t))
        hv = hv * lax.rsqrt(jnp.mean(hv * hv, axis=-1, keepdims=True) + RMS_EPS)
        gate = jax.nn.sigmoid(og_ref[:, h * M_V:(h + 1) * M_V].astype(f32))
        out_ref[:, h * M_V:(h + 1) * M_V] = (
            hv * nrm_ref[:, h * M_V:(h + 1) * M_V] * gate).astype(out_ref.dtype)

        a_col = g_tot - b_col + li_col
        m_loc = jnp.max(a_col, axis=0, keepdims=True)
        kw_t = (kf * jnp.exp(a_col - m_loc)).T.astype(bf16)
        c_loc = jnp.dot(kw_t, v_aug, preferred_element_type=f32)
        m_new = jnp.maximum(g_tot + m_prev, m_loc)
        c_scr[h] = jnp.exp(g_tot + m_prev - m_new) * c_prev + jnp.exp(m_loc - m_new) * c_loc
        m_scr[h] = jnp.broadcast_to(m_new, (1, LANES))


def _mlstm(proj, G, GT, conv_w, conv_b, m_norm):
    return pl.pallas_call(
        _mlstm_kernel,
        grid=(NBLK,),
        in_specs=[pl.BlockSpec((None, BLK, GROUP_W), lambda i: (0, i, 0)),
                  pl.BlockSpec((None, BLK, GROUP_W), lambda i: (1, i, 0)),
                  pl.BlockSpec((None, BLK, GROUP_W), lambda i: (2, i, 0)),
                  pl.BlockSpec((BLK, LANES), lambda i: (i, 0)),
                  pl.BlockSpec((8, BLK), lambda i: (0, i)),
                  pl.BlockSpec((CONV_W, 2 * M_QK_W), lambda i: (0, 0)),
                  pl.BlockSpec((1, 2 * M_QK_W), lambda i: (0, 0)),
                  pl.BlockSpec((1, M_V_W), lambda i: (0, 0))],
        out_specs=pl.BlockSpec((BLK, M_V_W), lambda i: (i, 0)),
        out_shape=jax.ShapeDtypeStruct((LP, M_V_W), bf16),
        scratch_shapes=[pltpu.VMEM((M_HEADS, M_QK, C_AUG), f32),
                        pltpu.VMEM((M_HEADS, 1, LANES), f32),
                        pltpu.VMEM((1, LANES), f32),
                        pltpu.VMEM((8, 2 * M_QK_W), f32)],
        compiler_params=pltpu.CompilerParams(
            dimension_semantics=("arbitrary",), vmem_limit_bytes=VMEM_LIMIT),
        name="mlstm",
    )(proj, proj, proj, G, GT, conv_w, conv_b, m_norm)


def _fox_kernel(qt_ref, qbt_ref, k_ref, kb_ref, vt_ref, nrm_ref, o_ref, m_scr, acc_scr):
    qi = pl.program_id(1)
    q_aug = jnp.concatenate([qt_ref[...], qbt_ref[...]], axis=0)
    m_scr[...] = jnp.full_like(m_scr, NEG)
    acc_scr[...] = jnp.zeros_like(acc_scr)

    ones_rows = jnp.ones((ONES_ROWS, BLK), bf16)

    def scores(c, start, masked):
        cols = slice(c * BLK, (c + 1) * BLK)
        k_aug = jnp.concatenate([k_ref[pl.ds(start, BLK), :], kb_ref[pl.ds(start, BLK), :]], axis=1)
        s = jnp.dot(k_aug, q_aug[:, cols], preferred_element_type=f32)
        if masked:
            kpos = start + lax.broadcasted_iota(jnp.int32, (BLK, 1), 0)
            qpos = qi * TQ + c * BLK + lax.broadcasted_iota(jnp.int32, (1, BLK), 1)
            s = jnp.where(kpos <= qpos, s, NEG)
        m_prev = m_scr[:, cols]
        m_new = jnp.maximum(m_prev, jnp.max(s, axis=0, keepdims=True))
        m_scr[:, cols] = m_new
        return s, m_prev, m_new

    def accumulate(c, start, s, m_prev, m_new):
        cols = slice(c * BLK, (c + 1) * BLK)
        v_aug = jnp.concatenate([vt_ref[:, pl.ds(start, BLK)], ones_rows], axis=0)
        p = jnp.exp2(s - m_new).astype(bf16)
        alpha = jnp.exp2(m_prev - m_new)
        acc_scr[:, cols] = alpha * acc_scr[:, cols] + jnp.dot(v_aug, p, preferred_element_type=f32)

    def run(units):
        pending = []
        for c, start, masked in units:
            pending.append((c, start) + scores(c, start, masked))
            if len(pending) > LOOKAHEAD:
                accumulate(*pending.pop(0))
        for unit in pending:
            accumulate(*unit)

    def full_blocks(first_key, n_blocks):
        run([(c, pl.multiple_of(first_key + d * BLK, BLK), False)
             for d in range(n_blocks) for c in range(SUB)])

    def body(j, carry):
        full_blocks(pl.multiple_of(j * (UNROLL * TQ), TQ), UNROLL * SUB)
        return carry

    lax.fori_loop(0, qi // UNROLL, body, 0)
    for r in range(1, UNROLL):
        @pl.when(qi % UNROLL >= r)
        def _():
            full_blocks(pl.multiple_of((qi // UNROLL * UNROLL + r - 1) * TQ, TQ), SUB)

    tail = pl.multiple_of(qi * TQ, TQ)
    run([(c, pl.multiple_of(tail + d * BLK, BLK), c == d) for d in range(SUB) for c in range(d, SUB)])

    acc = acc_scr[...]
    ot = acc[:F_HD, :] / acc[F_HD:F_HD + 1, :]
    ot = ot * lax.rsqrt(jnp.mean(ot * ot, axis=0, keepdims=True) + RMS_EPS)
    o_ref[...] = (ot.T * nrm_ref[...]).astype(o_ref.dtype)


def _fox(proj, proj_t, kb, qbt, f_norm):
    return pl.pallas_call(
        _fox_kernel,
        grid=(F_HEADS, LP // TQ),
        in_specs=[pl.BlockSpec((None, F_HD, TQ), lambda h, i: (0, h, i)),
                  pl.BlockSpec((None, LANES, TQ), lambda h, i: (h, 0, i)),
                  pl.BlockSpec((None, LP, F_HD), lambda h, i: (3, 0, h)),
                  pl.BlockSpec((None, LP, LANES), lambda h, i: (h, 0, 0)),
                  pl.BlockSpec((None, F_HD, LP), lambda h, i: (1, h, 0)),
                  pl.BlockSpec((None, 1, F_HD), lambda h, i: (h, 0, 0))],
        out_specs=pl.BlockSpec((TQ, F_HD), lambda h, i: (i, h)),
        out_shape=jax.ShapeDtypeStruct((LP, F_W), bf16),
        scratch_shapes=[pltpu.VMEM((1, TQ), f32),
                        pltpu.VMEM((F_HD + ONES_ROWS, TQ), f32)],
        compiler_params=pltpu.CompilerParams(
            dimension_semantics=("arbitrary", "arbitrary"), vmem_limit_bytes=VMEM_LIMIT),
        name="fox",
    )(proj_t, qbt, proj, kb, proj_t, f_norm)


def _stream_sources(h, head, x):
    if h is None:
        return head, x, 0
    return h, h, 1


def _stream_specs(rest, rest_off, skip):
    rest_block = lambda i: jnp.maximum(i + skip - 1, 0) + rest_off
    if rest.ndim == 3:
        rest_spec = pl.BlockSpec((None, BLK, D_MODEL), lambda i: (0, rest_block(i), 0))
    else:
        rest_spec = pl.BlockSpec((BLK, D_MODEL), lambda i: (rest_block(i), 0))
    return [pl.BlockSpec((BLK, D_MODEL), lambda i: (0, 0)), rest_spec]


def _outproj_kernel(hm_ref, hf_ref, w_ref, first_ref, rest_ref, g_ref, b_ref, of_ref, ob_ref, *, skip):
    half = BLK // 2
    for r in range(2):
        rows = slice(r * half, (r + 1) * half)
        y = jnp.dot(hm_ref[rows, :], w_ref[:M_V_W, :], preferred_element_type=f32)
        y = y + jnp.dot(hf_ref[rows, :], w_ref[M_V_W:, :], preferred_element_type=f32)
        res = rest_ref[rows, :]
        if skip == 0:
            res = jnp.where(pl.program_id(0) == 0, first_ref[rows, :], res)
        out = _layer_norm(DN_ALPHA * res + y, g_ref[...], b_ref[...])
        of_ref[rows, :] = out
        ob_ref[rows, :] = out.astype(bf16)


def _outproj(hm, hf, w, h, head, x, g, b, real_rows_only):
    skip = 1 if real_rows_only else 0
    first, rest, rest_off = _stream_sources(h, head, x)
    rows = LP - skip * BLK
    shifted = lambda i: (i + skip, 0)
    row = lambda i: (i, 0)
    const = lambda i: (0, 0)
    return pl.pallas_call(
        functools.partial(_outproj_kernel, skip=skip),
        grid=(rows // BLK,),
        in_specs=[pl.BlockSpec((BLK, M_V_W), shifted),
                  pl.BlockSpec((BLK, F_W), shifted),
                  pl.BlockSpec((M_V_W + F_W, D_MODEL), const)]
                 + _stream_specs(rest, rest_off, skip)
                 + [pl.BlockSpec((1, D_MODEL), const),
                    pl.BlockSpec((1, D_MODEL), const)],
        out_specs=[pl.BlockSpec((BLK, D_MODEL), row),
                   pl.BlockSpec((BLK, D_MODEL), row)],
        out_shape=[jax.ShapeDtypeStruct((rows, D_MODEL), f32),
                   jax.ShapeDtypeStruct((rows, D_MODEL), bf16)],
        compiler_params=pltpu.CompilerParams(
            dimension_semantics=("arbitrary",), vmem_limit_bytes=VMEM_LIMIT),
        name="outproj",
    )(hm, hf, w, first, rest, g, b)


def _mlp_kernel(xb_ref, wu_ref, wd_ref, h_ref, g_ref, b_ref, of_ref, ob_ref, acc_ref):
    f = pl.program_id(1)

    @pl.when(f == 0)
    def _():
        acc_ref[...] = jnp.zeros_like(acc_ref)

    u = jnp.maximum(jnp.dot(xb_ref[...], wu_ref[...], preferred_element_type=f32), 0.0)
    acc_ref[...] += jnp.dot((u * u).astype(bf16), wd_ref[...], preferred_element_type=f32)

    @pl.when(f == pl.num_programs(1) - 1)
    def _():
        out = _layer_norm(DN_ALPHA * h_ref[...] + acc_ref[...], g_ref[...], b_ref[...])
        of_ref[...] = out
        ob_ref[...] = out.astype(bf16)


def _mlp(xb, wu, wd, h, g, b, batched_out):
    rows = xb.shape[0]
    tm = rows // MLP_ROW_TILES
    assert tm * MLP_ROW_TILES == rows and tm % 8 == 0
    row = lambda i, f: (i, 0)
    const = lambda i, f: (0, 0)
    if batched_out:
        f32_spec = pl.BlockSpec((None, tm, D_MODEL), lambda i, f: (0, i, 0))
        f32_shape = jax.ShapeDtypeStruct((1, rows, D_MODEL), f32)
    else:
        f32_spec = pl.BlockSpec((tm, D_MODEL), row)
        f32_shape = jax.ShapeDtypeStruct((rows, D_MODEL), f32)
    return pl.pallas_call(
        _mlp_kernel,
        grid=(MLP_ROW_TILES, D_FF // TF),
        in_specs=[pl.BlockSpec((tm, D_MODEL), row),
                  pl.BlockSpec((D_MODEL, TF), lambda i, f: (0, f)),
                  pl.BlockSpec((TF, D_MODEL), lambda i, f: (f, 0)),
                  pl.BlockSpec((tm, D_MODEL), row),
                  pl.BlockSpec((1, D_MODEL), const),
                  pl.BlockSpec((1, D_MODEL), const)],
        out_specs=[f32_spec, pl.BlockSpec((tm, D_MODEL), row)],
        out_shape=[f32_shape, jax.ShapeDtypeStruct((rows, D_MODEL), bf16)],
        scratch_shapes=[pltpu.VMEM((tm, D_MODEL), f32)],
        compiler_params=pltpu.CompilerParams(
            dimension_semantics=("arbitrary", "arbitrary"), vmem_limit_bytes=VMEM_LIMIT),
        name="mlp",
    )(xb, wu, wd, h, g, b)


def _cast_kernel(x_ref, o_ref):
    o_ref[...] = x_ref[...].astype(o_ref.dtype)


def _to_bf16(w, l):
    _, rows, cols = w.shape
    tr = CAST_BLOCK_BYTES // (cols * 4)
    return pl.pallas_call(
        _cast_kernel,
        grid=(rows // tr,),
        in_specs=[pl.BlockSpec((None, tr, cols), lambda i: (l, i, 0))],
        out_specs=pl.BlockSpec((tr, cols), lambda i: (i, 0)),
        out_shape=jax.ShapeDtypeStruct((rows, cols), bf16),
        compiler_params=pltpu.CompilerParams(
            dimension_semantics=("arbitrary",), vmem_limit_bytes=VMEM_LIMIT),
        name="cast",
    )(w)


def _prep_w_in_kernel(w_ref, big_ref, bigt_ref, gate_ref):
    o_gate = 2 * M_QK_W + M_V_W
    o_mo = o_gate + 2 * M_HEADS
    o_fq = o_mo + M_V_W
    o_fk = o_fq + F_W
    o_fv = o_fk + F_W
    o_ff = o_fv + F_W
    assert o_gate % LANES == 0 and o_ff % LANES == 2 * M_HEADS
    gate_ref[...] = jnp.zeros_like(gate_ref)
    gate_ref[:, :2 * M_HEADS] = w_ref[:, o_gate:o_gate + 2 * M_HEADS]
    gate_ref[:, 2 * M_HEADS:2 * M_HEADS + F_HEADS] = w_ref[:, o_ff:o_ff + F_HEADS]
    n_lead = 2 * M_QK_W + M_V_W
    big_ref[:, :n_lead] = w_ref[:, :n_lead].astype(bf16)
    big_ref[:, n_lead:n_lead + M_V_W] = w_ref[:, o_mo:o_mo + M_V_W].astype(bf16)
    big_ref[:, n_lead + M_V_W:] = w_ref[:, o_fk:o_fk + F_W].astype(bf16)
    bigt_ref[0] = (w_ref[:, o_fq:o_fq + F_W] * (F_HD ** -0.5 * LOG2E)).T.astype(bf16)
    bigt_ref[1] = w_ref[:, o_fv:o_fv + F_W].T.astype(bf16)


def _prep_w_in(w_in, l):
    in_width = w_in.shape[2]
    return pl.pallas_call(
        _prep_w_in_kernel,
        grid=(D_MODEL // BLK,),
        in_specs=[pl.BlockSpec((None, BLK, in_width), lambda i: (l, i, 0))],
        out_specs=[pl.BlockSpec((BLK, N_GROUPS * GROUP_W), lambda i: (i, 0)),
                   pl.BlockSpec((N_GROUPS_T, GROUP_W, BLK), lambda i: (0, 0, i)),
                   pl.BlockSpec((BLK, LANES), lambda i: (i, 0))],
        out_shape=[jax.ShapeDtypeStruct((D_MODEL, N_GROUPS * GROUP_W), bf16),
                   jax.ShapeDtypeStruct((N_GROUPS_T, GROUP_W, D_MODEL), bf16),
                   jax.ShapeDtypeStruct((D_MODEL, LANES), f32)],
        compiler_params=pltpu.CompilerParams(
            dimension_semantics=("arbitrary",), vmem_limit_bytes=VMEM_LIMIT),
        name="prep_w_in",
    )(w_in)


def kernel(x, meta, w_in, conv_w, conv_b, m_i_bias, m_f_bias, m_norm, f_f_bias, f_norm, w_out,
           ln1_g, ln1_b, w_up, w_down, ln2_g, ln2_b):
    assert x.shape == (1, SEQ, D_MODEL)
    head = jnp.concatenate([jnp.zeros((P_PAD, D_MODEL), f32), meta.astype(f32)], axis=0)
    hb = jnp.concatenate([head.astype(bf16), x[0].astype(bf16)], axis=0)
    h = None
    n_gate = 2 * M_HEADS + F_HEADS
    for l in range(DEPTH):
        last = l == DEPTH - 1
        w_big, w_big_t, w_gate = _prep_w_in(w_in, l)
        b_gate = jnp.concatenate([m_i_bias[l], m_f_bias[l], f_f_bias[l],
                                  jnp.zeros((LANES - n_gate,), f32)])[None, :]
        G, GT, kb, qbt = _gates(h, head, x, w_gate, b_gate)
        proj = _inproj(hb, w_big)
        proj_t = _inproj_t(hb, w_big_t)
        hm = _mlstm(proj, G, GT, conv_w[l], conv_b[l][None, :], m_norm[l][None, :])
        hf = _fox(proj, proj_t, kb, qbt, f_norm[l].reshape(F_HEADS, 1, F_HD))
        h1, h1b = _outproj(hm, hf, _to_bf16(w_out, l), h, head, x,
                           ln1_g[l][None, :], ln1_b[l][None, :], real_rows_only=last)
        h, hb = _mlp(h1b, _to_bf16(w_up, l), _to_bf16(w_down, l), h1,
                     ln2_g[l][None, :], ln2_b[l][None, :], batched_out=last)
    return h
```

```python
import functools

import jax
import jax.numpy as jnp
from jax import lax
from jax.experimental import pallas as pl
from jax.experimental.pallas import tpu as pltpu

D_MODEL = 2048
SEQ = 16384
DEPTH = 2
N_META = 16
M_HEADS = 4
M_QK = 128
M_V = 256
CONV_W = 4
F_HEADS = 8
F_HD = 128
D_FF = 4 * D_MODEL
M_QK_W = M_HEADS * M_QK
M_V_W = M_HEADS * M_V
F_W = F_HEADS * F_HD
DN_ALPHA = (2 * DEPTH) ** 0.25
LN_EPS = 1e-5
RMS_EPS = 1e-6
NEG = -1e30

LANES = 128
BLK = 256
P_PAD = BLK - N_META
LP = P_PAD + N_META + SEQ
NBLK = LP // BLK
GROUP_W = 1024
W_GROUPS = 6
ROW_GROUPS = (0, 1, 2, 4)
COL_GROUPS = (3, 5)
N_GROUPS = len(ROW_GROUPS)
N_GROUPS_T = len(COL_GROUPS)
TQ = 1280
SUB = TQ // BLK
N_BIAS = 3
ONES_ROWS = 16
UNROLL = 2
LOOKAHEAD = 5
LOG2E = 1.4426950408889634
TM_PROJ = 1040
MLP_ROW_TILES = 32
TF = 1024
C_AUG = M_V + LANES
VMEM_LIMIT = 56 * 1024 * 1024
CAST_BLOCK_BYTES = 8 * 1024 * 1024

f32 = jnp.float32
bf16 = jnp.bfloat16


def _layer_norm(z, g, b):
    mu = jnp.mean(z, axis=-1, keepdims=True)
    zc = z - mu
    var = jnp.mean(zc * zc, axis=-1, keepdims=True)
    return zc * lax.rsqrt(var + LN_EPS) * g + b


def _log_sigmoid(x):
    return jnp.minimum(x, 0.0) - jnp.log(1.0 + jnp.exp(-jnp.abs(x)))


def _bf16_piece(x):
    return x.astype(bf16).astype(f32)


def _gates_kernel(first_ref, rest_ref, w_ref, b_ref, g_ref, gt_ref, kb_ref, qbt_ref,
                  carry_ref, wsplit_ref):
    i = pl.program_id(0)

    @pl.when(i == 0)
    def _():
        carry_ref[...] = jnp.zeros_like(carry_ref)
        w = w_ref[...]
        w_hi = _bf16_piece(w)
        wsplit_ref[:, :LANES] = w_hi.astype(bf16)
        wsplit_ref[:, LANES:] = (w - w_hi).astype(bf16)

    x = jnp.where(i == 0, first_ref[...], rest_ref[...])
    x_hi = _bf16_piece(x)
    both = jnp.dot(x_hi.astype(bf16), wsplit_ref[...], preferred_element_type=f32)
    low = jnp.dot((x - x_hi).astype(bf16), wsplit_ref[:, :LANES], preferred_element_type=f32)
    pre = both[:, :LANES] + (both[:, LANES:] + low) + b_ref[...]
    row = i * BLK + lax.broadcasted_iota(jnp.int32, (BLK, LANES), 0)
    col = lax.broadcasted_iota(jnp.int32, (BLK, LANES), 1)
    valid = row >= P_PAD
    is_f = (col >= M_HEADS) & (col < 2 * M_HEADS + F_HEADS)
    lf = jnp.where(valid & is_f, _log_sigmoid(pre), 0.0)
    r2 = lax.broadcasted_iota(jnp.int32, (BLK, BLK), 0)
    c2 = lax.broadcasted_iota(jnp.int32, (BLK, BLK), 1)
    tri = jnp.where(c2 <= r2, 1.0, 0.0).astype(f32)
    cum = jnp.dot(tri, lf, preferred_element_type=f32,
                  precision=lax.Precision.HIGHEST) + carry_ref[...]
    carry_ref[...] = cum[BLK - 1:BLK, :]
    out = jnp.where(col < M_HEADS, jnp.where(valid, pre, NEG), cum)
    g_ref[...] = out
    gt_ref[...] = out.T

    valid_col = i * BLK + lax.broadcasted_iota(jnp.int32, (BLK, 1), 0) >= P_PAD
    for h in range(F_HEADS):
        c = cum[:, 2 * M_HEADS + h:2 * M_HEADS + h + 1] * LOG2E
        pieces = []
        rest = c
        for _ in range(N_BIAS):
            piece = _bf16_piece(rest)
            pieces.append(piece)
            rest = rest - piece
        tile_k = jnp.where(col < N_BIAS, 1.0, 0.0)
        tile_q = jnp.where((col >= N_BIAS) & (col < 2 * N_BIAS), 1.0, 0.0)
        for n, piece in enumerate(pieces):
            key_piece = jnp.where(valid_col, -piece, NEG) if n == 0 else -piece
            tile_k = jnp.where(col == N_BIAS + n, key_piece, tile_k)
            tile_q = jnp.where(col == n, piece, tile_q)
        kb_ref[h] = tile_k.astype(bf16)
        qbt_ref[h] = tile_q.T.astype(bf16)


def _gates(h, head, x, wg, bg):
    first, rest, rest_off = _stream_sources(h, head, x)
    return pl.pallas_call(
        _gates_kernel,
        grid=(NBLK,),
        in_specs=_stream_specs(rest, rest_off, 0)
                 + [pl.BlockSpec((D_MODEL, LANES), lambda i: (0, 0)),
                    pl.BlockSpec((1, LANES), lambda i: (0, 0))],
        out_specs=[pl.BlockSpec((BLK, LANES), lambda i: (i, 0)),
                   pl.BlockSpec((LANES, BLK), lambda i: (0, i)),
                   pl.BlockSpec((F_HEADS, BLK, LANES), lambda i: (0, i, 0)),
                   pl.BlockSpec((F_HEADS, LANES, BLK), lambda i: (0, 0, i))],
        out_shape=[jax.ShapeDtypeStruct((LP, LANES), f32),
                   jax.ShapeDtypeStruct((LANES, LP), f32),
                   jax.ShapeDtypeStruct((F_HEADS, LP, LANES), bf16),
                   jax.ShapeDtypeStruct((F_HEADS, LANES, LP), bf16)],
        scratch_shapes=[pltpu.VMEM((1, LANES), f32),
                        pltpu.VMEM((D_MODEL, 2 * LANES), bf16)],
        compiler_params=pltpu.CompilerParams(dimension_semantics=("arbitrary",)),
        name="gates",
    )(first, rest, wg, bg)


def _inproj_kernel(x_ref, w_ref, o_ref):
    o_ref[...] = lax.dot_general(x_ref[...], w_ref[...], (((1,), (1,)), ((), ())),
                                 preferred_element_type=f32).astype(o_ref.dtype)


def _inproj(xb, wg):
    assert ROW_GROUPS == (0, 1, 2, 4)
    return pl.pallas_call(
        _inproj_kernel,
        grid=(N_GROUPS, LP // TM_PROJ),
        in_specs=[pl.BlockSpec((TM_PROJ, D_MODEL), lambda j, i: (i, 0)),
                  pl.BlockSpec((None, GROUP_W, D_MODEL),
                               lambda j, i: (j + (j >= 3).astype(jnp.int32), 0, 0))],
        out_specs=pl.BlockSpec((None, TM_PROJ, GROUP_W), lambda j, i: (j, i, 0)),
        out_shape=jax.ShapeDtypeStruct((N_GROUPS, LP, GROUP_W), bf16),
        compiler_params=pltpu.CompilerParams(
            dimension_semantics=("arbitrary", "arbitrary"), vmem_limit_bytes=VMEM_LIMIT),
        name="inproj",
    )(xb, wg)


def _inproj_t_kernel(x_ref, wt_ref, o_ref):
    o_ref[...] = lax.dot_general(wt_ref[...], x_ref[...], (((1,), (1,)), ((), ())),
                                 preferred_element_type=f32).astype(o_ref.dtype)


def _inproj_t(xb, wg):
    assert COL_GROUPS == (3, 5)
    return pl.pallas_call(
        _inproj_t_kernel,
        grid=(N_GROUPS_T, LP // TQ),
        in_specs=[pl.BlockSpec((TQ, D_MODEL), lambda j, i: (i, 0)),
                  pl.BlockSpec((None, GROUP_W, D_MODEL), lambda j, i: (3 + 2 * j, 0, 0))],
        out_specs=pl.BlockSpec((None, GROUP_W, TQ), lambda j, i: (j, 0, i)),
        out_shape=jax.ShapeDtypeStruct((N_GROUPS_T, GROUP_W, LP), bf16),
        compiler_params=pltpu.CompilerParams(
            dimension_semantics=("arbitrary", "arbitrary"), vmem_limit_bytes=VMEM_LIMIT),
        name="inproj_t",
    )(xb, wg)


def _mlstm_kernel(qk_ref, v_ref, og_ref, g_ref, gt_ref, cw_ref, cb_ref, nrm_ref, out_ref,
                  c_scr, m_scr, cf_scr, conv_scr):
    i = pl.program_id(0)
    T = BLK

    @pl.when(i == 0)
    def _():
        c_scr[...] = jnp.zeros_like(c_scr)
        m_scr[...] = jnp.zeros_like(m_scr)
        cf_scr[...] = jnp.zeros_like(cf_scr)
        conv_scr[...] = jnp.zeros_like(conv_scr)

    row = i * T + lax.broadcasted_iota(jnp.int32, (T, 1), 0)
    x = jnp.where(row >= P_PAD, qk_ref[...].astype(f32), 0.0)
    xs = jnp.concatenate([conv_scr[...], x], axis=0)
    cw = cw_ref[...]
    conv = cb_ref[...]
    for k in range(CONV_W):
        off = 8 - (CONV_W - 1) + k
        conv = conv + cw[k:k + 1, :] * xs[off:off + T, :]
    conv_scr[...] = x[T - 8:T, :]
    act = conv * jax.nn.sigmoid(conv)

    G = g_ref[...]
    GT = gt_ref[...]
    cfs_all = cf_scr[...]
    cf_scr[...] = G[T - 1:T, :]
    r2 = lax.broadcasted_iota(jnp.int32, (T, T), 0)
    c2 = lax.broadcasted_iota(jnp.int32, (T, T), 1)
    causal = c2 <= r2
    ones_blk = jnp.where(lax.broadcasted_iota(jnp.int32, (T, LANES), 1) == 0, 1.0, 0.0).astype(bf16)

    for h in range(M_HEADS):
        qh = (act[:, h * M_QK:(h + 1) * M_QK] * (M_QK ** -0.5)).astype(bf16)
        kf = act[:, M_QK_W + h * M_QK:M_QK_W + (h + 1) * M_QK]
        kh = kf.astype(bf16)
        v_aug = jnp.concatenate([v_ref[:, h * M_V:(h + 1) * M_V], ones_blk], axis=1)
        cfs = cfs_all[:, M_HEADS + h:M_HEADS + h + 1]
        b_col = G[:, M_HEADS + h:M_HEADS + h + 1] - cfs
        b_row = GT[M_HEADS + h:M_HEADS + h + 1, :] - cfs
        li_col = G[:, h:h + 1]
        li_row = GT[h:h + 1, :]
        g_tot = b_col[T - 1:T, :]
        m_prev = m_scr[h][:, 0:1]
        c_prev = c_scr[h]

        d = jnp.where(causal, b_col - b_row + li_row, NEG)
        inter_log = b_col + m_prev
        m_t = jnp.maximum(inter_log, jnp.max(d, axis=-1, keepdims=True))
        s = lax.dot_general(qh, kh, (((1,), (1,)), ((), ())), preferred_element_type=f32)
        w = (jnp.exp(d - m_t) * s).astype(bf16)
        s_inter = jnp.exp(inter_log - m_t)
        num = jnp.dot(w, v_aug, preferred_element_type=f32) + s_inter * jnp.dot(
            qh, c_prev.astype(bf16), preferred_element_type=f32)
        nq = num[:, M_V:M_V + 1]
        hv = num[:, :M_V] / jnp.maximum(jnp.abs(nq), jnp.exp(-m_t))
        hv = hv * lax.rsqrt(jnp.mean(hv * hv, axis=-1, keepdims=True) + RMS_EPS)
        gate = jax.nn.sigmoid(og_ref[:, h * M_V:(h + 1) * M_V].astype(f32))
        out_ref[:, h * M_V:(h + 1) * M_V] = (
            hv * nrm_ref[:, h * M_V:(h + 1) * M_V] * gate).astype(out_ref.dtype)

        a_col = g_tot - b_col + li_col
        m_loc = jnp.max(a_col, axis=0, keepdims=True)
        kw_t = (kf * jnp.exp(a_col - m_loc)).T.astype(bf16)
        c_loc = jnp.dot(kw_t, v_aug, preferred_element_type=f32)
        m_new = jnp.maximum(g_tot + m_prev, m_loc)
        c_scr[h] = jnp.exp(g_tot + m_prev - m_new) * c_prev + jnp.exp(m_loc - m_new) * c_loc
        m_scr[h] = jnp.broadcast_to(m_new, (1, LANES))


def _mlstm(proj, G, GT, conv_w, conv_b, m_norm):
    return pl.pallas_call(
        _mlstm_kernel,
        grid=(NBLK,),
        in_specs=[pl.BlockSpec((None, BLK, GROUP_W), lambda i: (0, i, 0)),
                  pl.BlockSpec((None, BLK, GROUP_W), lambda i: (1, i, 0)),
                  pl.BlockSpec((None, BLK, GROUP_W), lambda i: (2, i, 0)),
                  pl.BlockSpec((BLK, LANES), lambda i: (i, 0)),
                  pl.BlockSpec((8, BLK), lambda i: (0, i)),
                  pl.BlockSpec((CONV_W, 2 * M_QK_W), lambda i: (0, 0)),
                  pl.BlockSpec((1, 2 * M_QK_W), lambda i: (0, 0)),
                  pl.BlockSpec((1, M_V_W), lambda i: (0, 0))],
        out_specs=pl.BlockSpec((BLK, M_V_W), lambda i: (i, 0)),
        out_shape=jax.ShapeDtypeStruct((LP, M_V_W), bf16),
        scratch_shapes=[pltpu.VMEM((M_HEADS, M_QK, C_AUG), f32),
                        pltpu.VMEM((M_HEADS, 1, LANES), f32),
                        pltpu.VMEM((1, LANES), f32),
                        pltpu.VMEM((8, 2 * M_QK_W), f32)],
        compiler_params=pltpu.CompilerParams(
            dimension_semantics=("arbitrary",), vmem_limit_bytes=VMEM_LIMIT),
        name="mlstm",
    )(proj, proj, proj, G, GT, conv_w, conv_b, m_norm)


def _fox_kernel(qt_ref, qbt_ref, k_ref, kb_ref, vt_ref, nrm_ref, o_ref, m_scr, acc_scr):
    qi = pl.program_id(1)
    q_aug = jnp.concatenate([qt_ref[...], qbt_ref[...]], axis=0)
    m_scr[...] = jnp.full_like(m_scr, NEG)
    acc_scr[...] = jnp.zeros_like(acc_scr)

    ones_rows = jnp.ones((ONES_ROWS, BLK), bf16)

    def scores(c, start, masked):
        cols = slice(c * BLK, (c + 1) * BLK)
        k_aug = jnp.concatenate([k_ref[pl.ds(start, BLK), :], kb_ref[pl.ds(start, BLK), :]], axis=1)
        s = jnp.dot(k_aug, q_aug[:, cols], preferred_element_type=f32)
        if masked:
            kpos = start + lax.broadcasted_iota(jnp.int32, (BLK, 1), 0)
            qpos = qi * TQ + c * BLK + lax.broadcasted_iota(jnp.int32, (1, BLK), 1)
            s = jnp.where(kpos <= qpos, s, NEG)
        m_prev = m_scr[:, cols]
        m_new = jnp.maximum(m_prev, jnp.max(s, axis=0, keepdims=True))
        m_scr[:, cols] = m_new
        return s, m_prev, m_new

    def accumulate(c, start, s, m_prev, m_new):
        cols = slice(c * BLK, (c + 1) * BLK)
        v_aug = jnp.concatenate([vt_ref[:, pl.ds(start, BLK)], ones_rows], axis=0)
        p = jnp.exp2(s - m_new).astype(bf16)
        alpha = jnp.exp2(m_prev - m_new)
        acc_scr[:, cols] = alpha * acc_scr[:, cols] + jnp.dot(v_aug, p, preferred_element_type=f32)

    def run(units):
        pending = []
        for c, start, masked in units:
            pending.append((c, start) + scores(c, start, masked))
            if len(pending) > LOOKAHEAD:
                accumulate(*pending.pop(0))
        for unit in pending:
            accumulate(*unit)

    def full_blocks(first_key, n_blocks):
        run([(c, pl.multiple_of(first_key + d * BLK, BLK), False)
             for d in range(n_blocks) for c in range(SUB)])

    def body(j, carry):
        full_blocks(pl.multiple_of(j * (UNROLL * TQ), TQ), UNROLL * SUB)
        return carry

    lax.fori_loop(0, qi // UNROLL, body, 0)
    for r in range(1, UNROLL):
        @pl.when(qi % UNROLL >= r)
        def _():
            full_blocks(pl.multiple_of((qi // UNROLL * UNROLL + r - 1) * TQ, TQ), SUB)

    tail = pl.multiple_of(qi * TQ, TQ)
    run([(c, pl.multiple_of(tail + d * BLK, BLK), c == d) for d in range(SUB) for c in range(d, SUB)])

    acc = acc_scr[...]
    ot = acc[:F_HD, :] / acc[F_HD:F_HD + 1, :]
    ot = ot * lax.rsqrt(jnp.mean(ot * ot, axis=0, keepdims=True) + RMS_EPS)
    o_ref[...] = (ot.T * nrm_ref[...]).astype(o_ref.dtype)


def _fox(proj, proj_t, kb, qbt, f_norm):
    return pl.pallas_call(
        _fox_kernel,
        grid=(F_HEADS, LP // TQ),
        in_specs=[pl.BlockSpec((None, F_HD, TQ), lambda h, i: (0, h, i)),
                  pl.BlockSpec((None, LANES, TQ), lambda h, i: (h, 0, i)),
                  pl.BlockSpec((None, LP, F_HD), lambda h, i: (3, 0, h)),
                  pl.BlockSpec((None, LP, LANES), lambda h, i: (h, 0, 0)),
                  pl.BlockSpec((None, F_HD, LP), lambda h, i: (1, h, 0)),
                  pl.BlockSpec((None, 1, F_HD), lambda h, i: (h, 0, 0))],
        out_specs=pl.BlockSpec((TQ, F_HD), lambda h, i: (i, h)),
        out_shape=jax.ShapeDtypeStruct((LP, F_W), bf16),
        scratch_shapes=[pltpu.VMEM((1, TQ), f32),
                        pltpu.VMEM((F_HD + ONES_ROWS, TQ), f32)],
        compiler_params=pltpu.CompilerParams(
            dimension_semantics=("arbitrary", "arbitrary"), vmem_limit_bytes=VMEM_LIMIT),
        name="fox",
    )(proj_t, qbt, proj, kb, proj_t, f_norm)


def _stream_sources(h, head, x):
    if h is None:
        return head, x, 0
    return h, h, 1


def _stream_specs(rest, rest_off, skip):
    rest_block = lambda i: jnp.maximum(i + skip - 1, 0) + rest_off
    if rest.ndim == 3:
        rest_spec = pl.BlockSpec((None, BLK, D_MODEL), lambda i: (0, rest_block(i), 0))
    else:
        rest_spec = pl.BlockSpec((BLK, D_MODEL), lambda i: (rest_block(i), 0))
    return [pl.BlockSpec((BLK, D_MODEL), lambda i: (0, 0)), rest_spec]


def _outproj_kernel(hm_ref, hf_ref, w_ref, first_ref, rest_ref, g_ref, b_ref, of_ref, ob_ref, *, skip):
    half = BLK // 2
    for r in range(2):
        rows = slice(r * half, (r + 1) * half)
        y = jnp.dot(hm_ref[rows, :], w_ref[:M_V_W, :], preferred_element_type=f32)
        y = y + jnp.dot(hf_ref[rows, :], w_ref[M_V_W:, :], preferred_element_type=f32)
        res = rest_ref[rows, :]
        if skip == 0:
            res = jnp.where(pl.program_id(0) == 0, first_ref[rows, :], res)
        out = _layer_norm(DN_ALPHA * res + y, g_ref[...], b_ref[...])
        of_ref[rows, :] = out
        ob_ref[rows, :] = out.astype(bf16)


def _outproj(hm, hf, w, h, head, x, g, b, real_rows_only):
    skip = 1 if real_rows_only else 0
    first, rest, rest_off = _stream_sources(h, head, x)
    rows = LP - skip * BLK
    shifted = lambda i: (i + skip, 0)
    row = lambda i: (i, 0)
    const = lambda i: (0, 0)
    return pl.pallas_call(
        functools.partial(_outproj_kernel, skip=skip),
        grid=(rows // BLK,),
        in_specs=[pl.BlockSpec((BLK, M_V_W), shifted),
                  pl.BlockSpec((BLK, F_W), shifted),
                  pl.BlockSpec((M_V_W + F_W, D_MODEL), const)]
                 + _stream_specs(rest, rest_off, skip)
                 + [pl.BlockSpec((1, D_MODEL), const),
                    pl.BlockSpec((1, D_MODEL), const)],
        out_specs=[pl.BlockSpec((BLK, D_MODEL), row),
                   pl.BlockSpec((BLK, D_MODEL), row)],
        out_shape=[jax.ShapeDtypeStruct((rows, D_MODEL), f32),
                   jax.ShapeDtypeStruct((rows, D_MODEL), bf16)],
        compiler_params=pltpu.CompilerParams(
            dimension_semantics=("arbitrary",), vmem_limit_bytes=VMEM_LIMIT),
        name="outproj",
    )(hm, hf, w, first, rest, g, b)


def _mlp_kernel(xb_ref, wu_ref, wd_ref, h_ref, g_ref, b_ref, of_ref, ob_ref, acc_ref):
    f = pl.program_id(1)

    @pl.when(f == 0)
    def _():
        acc_ref[...] = jnp.zeros_like(acc_ref)

    u = jnp.maximum(jnp.dot(xb_ref[...], wu_ref[...], preferred_element_type=f32), 0.0)
    acc_ref[...] += jnp.dot((u * u).astype(bf16), wd_ref[...], preferred_element_type=f32)

    @pl.when(f == pl.num_programs(1) - 1)
    def _():
        out = _layer_norm(DN_ALPHA * h_ref[...] + acc_ref[...], g_ref[...], b_ref[...])
        of_ref[...] = out
        ob_ref[...] = out.astype(bf16)


def _mlp(xb, wu, wd, h, g, b, batched_out):
    rows = xb.shape[0]
    tm = rows // MLP_ROW_TILES
    assert tm * MLP_ROW_TILES == rows and tm % 8 == 0
    row = lambda i, f: (i, 0)
    const = lambda i, f: (0, 0)
    if batched_out:
        f32_spec = pl.BlockSpec((None, tm, D_MODEL), lambda i, f: (0, i, 0))
        f32_shape = jax.ShapeDtypeStruct((1, rows, D_MODEL), f32)
    else:
        f32_spec = pl.BlockSpec((tm, D_MODEL), row)
        f32_shape = jax.ShapeDtypeStruct((rows, D_MODEL), f32)
    return pl.pallas_call(
        _mlp_kernel,
        grid=(MLP_ROW_TILES, D_FF // TF),
        in_specs=[pl.BlockSpec((tm, D_MODEL), row),
                  pl.BlockSpec((D_MODEL, TF), lambda i, f: (0, f)),
                  pl.BlockSpec((TF, D_MODEL), lambda i, f: (f, 0)),
                  pl.BlockSpec((tm, D_MODEL), row),
                  pl.BlockSpec((1, D_MODEL), const),
                  pl.BlockSpec((1, D_MODEL), const)],
        out_specs=[f32_spec, pl.BlockSpec((tm, D_MODEL), row)],
        out_shape=[f32_shape, jax.ShapeDtypeStruct((rows, D_MODEL), bf16)],
        scratch_shapes=[pltpu.VMEM((tm, D_MODEL), f32)],
        compiler_params=pltpu.CompilerParams(
            dimension_semantics=("arbitrary", "arbitrary"), vmem_limit_bytes=VMEM_LIMIT),
        name="mlp",
    )(xb, wu, wd, h, g, b)


def _cast_kernel(x_ref, o_ref):
    o_ref[...] = x_ref[...].astype(o_ref.dtype)


def _to_bf16(w, l):
    _, rows, cols = w.shape
    tr = CAST_BLOCK_BYTES // (cols * 4)
    return pl.pallas_call(
        _cast_kernel,
        grid=(rows // tr,),
        in_specs=[pl.BlockSpec((None, tr, cols), lambda i: (l, i, 0))],
        out_specs=pl.BlockSpec((tr, cols), lambda i: (i, 0)),
        out_shape=jax.ShapeDtypeStruct((rows, cols), bf16),
        compiler_params=pltpu.CompilerParams(
            dimension_semantics=("arbitrary",), vmem_limit_bytes=VMEM_LIMIT),
        name="cast",
    )(w)


def _prep_w_in_kernel(wt_ref, wg_ref):
    scale = jnp.where(pl.program_id(0) == COL_GROUPS[0], F_HD ** -0.5 * LOG2E, 1.0)
    wg_ref[...] = (wt_ref[0] * scale).astype(bf16)


def _prep_w_in(w_in, l):
    wt = jnp.swapaxes(w_in, 1, 2)
    lead_groups = 2
    gate_rows = 2 * M_HEADS
    sub = 8
    assert GROUP_W % sub == 0 and gate_rows % sub == 0
    first_row = lambda g: sub * (g * (GROUP_W // sub)
                                 + (gate_rows // sub) * (g >= lead_groups).astype(jnp.int32))
    wg = pl.pallas_call(
        _prep_w_in_kernel,
        grid=(W_GROUPS,),
        in_specs=[pl.BlockSpec((pl.Element(1), pl.Element(GROUP_W), pl.Element(D_MODEL)),
                               lambda g: (l, first_row(g), 0))],
        out_specs=pl.BlockSpec((None, GROUP_W, D_MODEL), lambda g: (g, 0, 0)),
        out_shape=jax.ShapeDtypeStruct((W_GROUPS, GROUP_W, D_MODEL), bf16),
        compiler_params=pltpu.CompilerParams(
            dimension_semantics=("arbitrary",), vmem_limit_bytes=VMEM_LIMIT),
        name="prep_w_in",
    )(wt)
    o_gate = lead_groups * GROUP_W
    o_ff = W_GROUPS * GROUP_W + gate_rows
    assert o_ff + F_HEADS == w_in.shape[2]
    gate_t = jnp.concatenate([wt[l, o_gate:o_gate + gate_rows], wt[l, o_ff:o_ff + F_HEADS]], axis=0)
    gate = jnp.zeros((D_MODEL, LANES), f32).at[:, :gate_rows + F_HEADS].set(gate_t.T)
    return wg, gate


def kernel(x, meta, w_in, conv_w, conv_b, m_i_bias, m_f_bias, m_norm, f_f_bias, f_norm, w_out,
           ln1_g, ln1_b, w_up, w_down, ln2_g, ln2_b):
    assert x.shape == (1, SEQ, D_MODEL)
    head = jnp.concatenate([jnp.zeros((P_PAD, D_MODEL), f32), meta.astype(f32)], axis=0)
    hb = jnp.concatenate([head.astype(bf16), x[0].astype(bf16)], axis=0)
    h = None
    n_gate = 2 * M_HEADS + F_HEADS
    for l in range(DEPTH):
        last = l == DEPTH - 1
        w_groups, w_gate = _prep_w_in(w_in, l)
        b_gate = jnp.concatenate([m_i_bias[l], m_f_bias[l], f_f_bias[l],
                                  jnp.zeros((LANES - n_gate,), f32)])[None, :]
        G, GT, kb, qbt = _gates(h, head, x, w_gate, b_gate)
        proj = _inproj(hb, w_groups)
        proj_t = _inproj_t(hb, w_groups)
        hm = _mlstm(proj, G, GT, conv_w[l], conv_b[l][None, :], m_norm[l][None, :])
        hf = _fox(proj, proj_t, kb, qbt, f_norm[l].reshape(F_HEADS, 1, F_HD))
        h1, h1b = _outproj(hm, hf, _to_bf16(w_out, l), h, head, x,
                           ln1_g[l][None, :], ln1_b[l][None, :], real_rows_only=last)
        h, hb = _mlp(h1b, _to_bf16(w_up, l), _to_bf16(w_down, l), h1,
                     ln2_g[l][None, :], ln2_b[l][None, :], batched_out=last)
    return h
```

```python
import functools

import jax
import jax.numpy as jnp
from jax import lax
from jax.experimental import pallas as pl
from jax.experimental.pallas import tpu as pltpu

D_MODEL = 2048
SEQ = 16384
DEPTH = 2
N_META = 16
M_HEADS = 4
M_QK = 128
M_V = 256
CONV_W = 4
F_HEADS = 8
F_HD = 128
D_FF = 4 * D_MODEL
M_QK_W = M_HEADS * M_QK
M_V_W = M_HEADS * M_V
F_W = F_HEADS * F_HD
DN_ALPHA = (2 * DEPTH) ** 0.25
LN_EPS = 1e-5
RMS_EPS = 1e-6
NEG = -1e30

LANES = 128
BLK = 256
P_PAD = BLK - N_META
LP = P_PAD + N_META + SEQ
NBLK = LP // BLK
GROUP_W = 1024
W_GROUPS = 6
ROW_GROUPS = (0, 1, 2, 4)
COL_GROUPS = (3, 5)
N_GROUPS = len(ROW_GROUPS)
N_GROUPS_T = len(COL_GROUPS)
TQ = 1280
SUB = TQ // BLK
N_BIAS = 3
ONES_ROWS = 16
UNROLL = 2
LOOKAHEAD = 4
LOG2E = 1.4426950408889634
TM_PROJ = 1040
MLP_ROW_TILES = 32
TF = 1024
C_AUG = M_V + LANES
VMEM_LIMIT = 56 * 1024 * 1024
CAST_BLOCK_BYTES = 8 * 1024 * 1024

f32 = jnp.float32
bf16 = jnp.bfloat16


def _layer_norm(z, g, b):
    mu = jnp.mean(z, axis=-1, keepdims=True)
    zc = z - mu
    var = jnp.mean(zc * zc, axis=-1, keepdims=True)
    return zc * lax.rsqrt(var + LN_EPS) * g + b


def _log_sigmoid(x):
    return jnp.minimum(x, 0.0) - jnp.log(1.0 + jnp.exp(-jnp.abs(x)))


def _bf16_piece(x):
    return x.astype(bf16).astype(f32)


def _gates_kernel(first_ref, rest_ref, w_ref, b_ref, g_ref, gt_ref, kb_ref, qbt_ref,
                  carry_ref, wsplit_ref):
    i = pl.program_id(0)

    @pl.when(i == 0)
    def _():
        carry_ref[...] = jnp.zeros_like(carry_ref)
        w = w_ref[...]
        w_hi = _bf16_piece(w)
        wsplit_ref[:, :LANES] = w_hi.astype(bf16)
        wsplit_ref[:, LANES:] = (w - w_hi).astype(bf16)

    x = jnp.where(i == 0, first_ref[...], rest_ref[...])
    x_hi = _bf16_piece(x)
    both = jnp.dot(x_hi.astype(bf16), wsplit_ref[...], preferred_element_type=f32)
    low = jnp.dot((x - x_hi).astype(bf16), wsplit_ref[:, :LANES], preferred_element_type=f32)
    pre = both[:, :LANES] + (both[:, LANES:] + low) + b_ref[...]
    row = i * BLK + lax.broadcasted_iota(jnp.int32, (BLK, LANES), 0)
    col = lax.broadcasted_iota(jnp.int32, (BLK, LANES), 1)
    valid = row >= P_PAD
    is_f = (col >= M_HEADS) & (col < 2 * M_HEADS + F_HEADS)
    lf = jnp.where(valid & is_f, _log_sigmoid(pre), 0.0)
    r2 = lax.broadcasted_iota(jnp.int32, (BLK, BLK), 0)
    c2 = lax.broadcasted_iota(jnp.int32, (BLK, BLK), 1)
    tri = jnp.where(c2 <= r2, 1.0, 0.0).astype(f32)
    cum = jnp.dot(tri, lf, preferred_element_type=f32,
                  precision=lax.Precision.HIGHEST) + carry_ref[...]
    carry_ref[...] = cum[BLK - 1:BLK, :]
    out = jnp.where(col < M_HEADS, jnp.where(valid, pre, NEG), cum)
    g_ref[...] = out
    gt_ref[...] = out.T

    valid_col = i * BLK + lax.broadcasted_iota(jnp.int32, (BLK, 1), 0) >= P_PAD
    for h in range(F_HEADS):
        c = cum[:, 2 * M_HEADS + h:2 * M_HEADS + h + 1] * LOG2E
        pieces = []
        rest = c
        for _ in range(N_BIAS):
            piece = _bf16_piece(rest)
            pieces.append(piece)
            rest = rest - piece
        tile_k = jnp.where(col < N_BIAS, 1.0, 0.0)
        tile_q = jnp.where((col >= N_BIAS) & (col < 2 * N_BIAS), 1.0, 0.0)
        for n, piece in enumerate(pieces):
            key_piece = jnp.where(valid_col, -piece, NEG) if n == 0 else -piece
            tile_k = jnp.where(col == N_BIAS + n, key_piece, tile_k)
            tile_q = jnp.where(col == n, piece, tile_q)
        kb_ref[h] = tile_k.astype(bf16)
        qbt_ref[h] = tile_q.T.astype(bf16)


def _gates(h, head, x, wg, bg):
    first, rest, rest_off = _stream_sources(h, head, x)
    return pl.pallas_call(
        _gates_kernel,
        grid=(NBLK,),
        in_specs=_stream_specs(rest, rest_off, 0)
                 + [pl.BlockSpec((D_MODEL, LANES), lambda i: (0, 0)),
                    pl.BlockSpec((1, LANES), lambda i: (0, 0))],
        out_specs=[pl.BlockSpec((BLK, LANES), lambda i: (i, 0)),
                   pl.BlockSpec((LANES, BLK), lambda i: (0, i)),
                   pl.BlockSpec((F_HEADS, BLK, LANES), lambda i: (0, i, 0)),
                   pl.BlockSpec((F_HEADS, LANES, BLK), lambda i: (0, 0, i))],
        out_shape=[jax.ShapeDtypeStruct((LP, LANES), f32),
                   jax.ShapeDtypeStruct((LANES, LP), f32),
                   jax.ShapeDtypeStruct((F_HEADS, LP, LANES), bf16),
                   jax.ShapeDtypeStruct((F_HEADS, LANES, LP), bf16)],
        scratch_shapes=[pltpu.VMEM((1, LANES), f32),
                        pltpu.VMEM((D_MODEL, 2 * LANES), bf16)],
        compiler_params=pltpu.CompilerParams(dimension_semantics=("arbitrary",)),
        name="gates",
    )(first, rest, wg, bg)


def _inproj_kernel(x_ref, w_ref, o_ref):
    o_ref[...] = lax.dot_general(x_ref[...], w_ref[...], (((1,), (1,)), ((), ())),
                                 preferred_element_type=f32).astype(o_ref.dtype)


def _inproj(xb, wg):
    assert ROW_GROUPS == (0, 1, 2, 4)
    return pl.pallas_call(
        _inproj_kernel,
        grid=(N_GROUPS, LP // TM_PROJ),
        in_specs=[pl.BlockSpec((TM_PROJ, D_MODEL), lambda j, i: (i, 0)),
                  pl.BlockSpec((None, GROUP_W, D_MODEL),
                               lambda j, i: (j + (j >= 3).astype(jnp.int32), 0, 0))],
        out_specs=pl.BlockSpec((None, TM_PROJ, GROUP_W), lambda j, i: (j, i, 0)),
        out_shape=jax.ShapeDtypeStruct((N_GROUPS, LP, GROUP_W), bf16),
        compiler_params=pltpu.CompilerParams(
            dimension_semantics=("arbitrary", "arbitrary"), vmem_limit_bytes=VMEM_LIMIT),
        name="inproj",
    )(xb, wg)


def _inproj_t_kernel(x_ref, wt_ref, o_ref):
    o_ref[...] = lax.dot_general(wt_ref[...], x_ref[...], (((1,), (1,)), ((), ())),
                                 preferred_element_type=f32).astype(o_ref.dtype)


def _inproj_t(xb, wg):
    assert COL_GROUPS == (3, 5)
    return pl.pallas_call(
        _inproj_t_kernel,
        grid=(N_GROUPS_T, LP // TQ),
        in_specs=[pl.BlockSpec((TQ, D_MODEL), lambda j, i: (i, 0)),
                  pl.BlockSpec((None, GROUP_W, D_MODEL), lambda j, i: (3 + 2 * j, 0, 0))],
        out_specs=pl.BlockSpec((None, GROUP_W, TQ), lambda j, i: (j, 0, i)),
        out_shape=jax.ShapeDtypeStruct((N_GROUPS_T, GROUP_W, LP), bf16),
        compiler_params=pltpu.CompilerParams(
            dimension_semantics=("arbitrary", "arbitrary"), vmem_limit_bytes=VMEM_LIMIT),
        name="inproj_t",
    )(xb, wg)


def _mlstm_kernel(qk_ref, v_ref, og_ref, g_ref, gt_ref, cw_ref, cb_ref, nrm_ref, out_ref,
                  c_scr, m_scr, cf_scr, conv_scr):
    i = pl.program_id(0)
    T = BLK

    @pl.when(i == 0)
    def _():
        c_scr[...] = jnp.zeros_like(c_scr)
        m_scr[...] = jnp.zeros_like(m_scr)
        cf_scr[...] = jnp.zeros_like(cf_scr)
        conv_scr[...] = jnp.zeros_like(conv_scr)

    row = i * T + lax.broadcasted_iota(jnp.int32, (T, 1), 0)
    x = jnp.where(row >= P_PAD, qk_ref[...].astype(f32), 0.0)
    xs = jnp.concatenate([conv_scr[...], x], axis=0)
    cw = cw_ref[...]
    conv = cb_ref[...]
    for k in range(CONV_W):
        off = 8 - (CONV_W - 1) + k
        conv = conv + cw[k:k + 1, :] * xs[off:off + T, :]
    conv_scr[...] = x[T - 8:T, :]
    act = conv * jax.nn.sigmoid(conv)

    G = g_ref[...]
    GT = gt_ref[...]
    cfs_all = cf_scr[...]
    cf_scr[...] = G[T - 1:T, :]
    r2 = lax.broadcasted_iota(jnp.int32, (T, T), 0)
    c2 = lax.broadcasted_iota(jnp.int32, (T, T), 1)
    causal = c2 <= r2
    ones_blk = jnp.where(lax.broadcasted_iota(jnp.int32, (T, LANES), 1) == 0, 1.0, 0.0).astype(bf16)

    for h in range(M_HEADS):
        qh = (act[:, h * M_QK:(h + 1) * M_QK] * (M_QK ** -0.5)).astype(bf16)
        kf = act[:, M_QK_W + h * M_QK:M_QK_W + (h + 1) * M_QK]
        kh = kf.astype(bf16)
        v_aug = jnp.concatenate([v_ref[:, h * M_V:(h + 1) * M_V], ones_blk], axis=1)
        cfs = cfs_all[:, M_HEADS + h:M_HEADS + h + 1]
        b_col = G[:, M_HEADS + h:M_HEADS + h + 1] - cfs
        b_row = GT[M_HEADS + h:M_HEADS + h + 1, :] - cfs
        li_col = G[:, h:h + 1]
        li_row = GT[h:h + 1, :]
        g_tot = b_col[T - 1:T, :]
        m_prev = m_scr[h][:, 0:1]
        c_prev = c_scr[h]

        d = jnp.where(causal, b_col - b_row + li_row, NEG)
        inter_log = b_col + m_prev
        m_t = jnp.maximum(inter_log, jnp.max(d, axis=-1, keepdims=True))
        s = lax.dot_general(qh, kh, (((1,), (1,)), ((), ())), preferred_element_type=f32)
        w = (jnp.exp(d - m_t) * s).astype(bf16)
        s_inter = jnp.exp(inter_log - m_t)
        num = jnp.dot(w, v_aug, preferred_element_type=f32) + s_inter * jnp.dot(
            qh, c_prev.astype(bf16), preferred_element_type=f32)
        nq = num[:, M_V:M_V + 1]
        hv = num[:, :M_V] / jnp.maximum(jnp.abs(nq), jnp.exp(-m_t))
        hv = hv * lax.rsqrt(jnp.mean(hv * hv, axis=-1, keepdims=True) + RMS_EPS)
        gate = jax.nn.sigmoid(og_ref[:, h * M_V:(h + 1) * M_V].astype(f32))
        out_ref[:, h * M_V:(h + 1) * M_V] = (
            hv * nrm_ref[:, h * M_V:(h + 1) * M_V] * gate).astype(out_ref.dtype)

        a_col = g_tot - b_col + li_col
        m_loc = jnp.max(a_col, axis=0, keepdims=True)
        kw_t = (kf * jnp.exp(a_col - m_loc)).T.astype(bf16)
        c_loc = jnp.dot(kw_t, v_aug, preferred_element_type=f32)
        m_new = jnp.maximum(g_tot + m_prev, m_loc)
        c_scr[h] = jnp.exp(g_tot + m_prev - m_new) * c_prev + jnp.exp(m_loc - m_new) * c_loc
        m_scr[h] = jnp.broadcast_to(m_new, (1, LANES))


def _mlstm(proj, G, GT, conv_w, conv_b, m_norm):
    return pl.pallas_call(
        _mlstm_kernel,
        grid=(NBLK,),
        in_specs=[pl.BlockSpec((None, BLK, GROUP_W), lambda i: (0, i, 0)),
                  pl.BlockSpec((None, BLK, GROUP_W), lambda i: (1, i, 0)),
                  pl.BlockSpec((None, BLK, GROUP_W), lambda i: (2, i, 0)),
                  pl.BlockSpec((BLK, LANES), lambda i: (i, 0)),
                  pl.BlockSpec((8, BLK), lambda i: (0, i)),
                  pl.BlockSpec((CONV_W, 2 * M_QK_W), lambda i: (0, 0)),
                  pl.BlockSpec((1, 2 * M_QK_W), lambda i: (0, 0)),
                  pl.BlockSpec((1, M_V_W), lambda i: (0, 0))],
        out_specs=pl.BlockSpec((BLK, M_V_W), lambda i: (i, 0)),
        out_shape=jax.ShapeDtypeStruct((LP, M_V_W), bf16),
        scratch_shapes=[pltpu.VMEM((M_HEADS, M_QK, C_AUG), f32),
                        pltpu.VMEM((M_HEADS, 1, LANES), f32),
                        pltpu.VMEM((1, LANES), f32),
                        pltpu.VMEM((8, 2 * M_QK_W), f32)],
        compiler_params=pltpu.CompilerParams(
            dimension_semantics=("arbitrary",), vmem_limit_bytes=VMEM_LIMIT),
        name="mlstm",
    )(proj, proj, proj, G, GT, conv_w, conv_b, m_norm)


def _fox_kernel(qt_ref, qbt_ref, k_ref, kb_ref, vt_ref, nrm_ref, o_ref, m_scr, acc_scr):
    qi = pl.program_id(1)
    q_aug = jnp.concatenate([qt_ref[...], qbt_ref[...]], axis=0)
    m_scr[...] = jnp.full_like(m_scr, NEG)
    acc_scr[...] = jnp.zeros_like(acc_scr)

    ones_rows = jnp.ones((ONES_ROWS, BLK), bf16)

    def scores(c, start, masked):
        cols = slice(c * BLK, (c + 1) * BLK)
        k_aug = jnp.concatenate([k_ref[pl.ds(start, BLK), :], kb_ref[pl.ds(start, BLK), :]], axis=1)
        s = jnp.dot(k_aug, q_aug[:, cols], preferred_element_type=f32)
        if masked:
            kpos = start + lax.broadcasted_iota(jnp.int32, (BLK, 1), 0)
            qpos = qi * TQ + c * BLK + lax.broadcasted_iota(jnp.int32, (1, BLK), 1)
            s = jnp.where(kpos <= qpos, s, NEG)
        m_prev = m_scr[:, cols]
        m_new = jnp.maximum(m_prev, jnp.max(s, axis=0, keepdims=True))
        m_scr[:, cols] = m_new
        return s, m_prev, m_new

    def accumulate(c, start, s, m_prev, m_new):
        cols = slice(c * BLK, (c + 1) * BLK)
        v_aug = jnp.concatenate([vt_ref[:, pl.ds(start, BLK)], ones_rows], axis=0)
        p = jnp.exp2(s - m_new).astype(bf16)
        alpha = jnp.exp2(m_prev - m_new)
        acc_scr[:, cols] = alpha * acc_scr[:, cols] + jnp.dot(v_aug, p, preferred_element_type=f32)

    def run(units):
        pending = []
        for c, start, masked in units:
            pending.append((c, start) + scores(c, start, masked))
            if len(pending) > LOOKAHEAD:
                accumulate(*pending.pop(0))
        for unit in pending:
            accumulate(*unit)

    def full_blocks(first_key, n_blocks):
        run([(c, pl.multiple_of(first_key + d * BLK, BLK), False)
             for d in range(n_blocks) for c in range(SUB)])

    def body(j, carry):
        full_blocks(pl.multiple_of(j * (UNROLL * TQ), TQ), UNROLL * SUB)
        return carry

    lax.fori_loop(0, qi // UNROLL, body, 0)
    for r in range(1, UNROLL):
        @pl.when(qi % UNROLL >= r)
        def _():
            full_blocks(pl.multiple_of((qi // UNROLL * UNROLL + r - 1) * TQ, TQ), SUB)

    tail = pl.multiple_of(qi * TQ, TQ)
    run([(c, pl.multiple_of(tail + d * BLK, BLK), c == d) for d in range(SUB) for c in range(d, SUB)])

    acc = acc_scr[...]
    ot = acc[:F_HD, :] / acc[F_HD:F_HD + 1, :]
    ot = ot * lax.rsqrt(jnp.mean(ot * ot, axis=0, keepdims=True) + RMS_EPS)
    o_ref[...] = (ot.T * nrm_ref[...]).astype(o_ref.dtype)


def _fox(proj, proj_t, kb, qbt, f_norm):
    return pl.pallas_call(
        _fox_kernel,
        grid=(F_HEADS, LP // TQ),
        in_specs=[pl.BlockSpec((None, F_HD, TQ), lambda h, i: (0, h, i)),
                  pl.BlockSpec((None, LANES, TQ), lambda h, i: (h, 0, i)),
                  pl.BlockSpec((None, LP, F_HD), lambda h, i: (3, 0, h)),
                  pl.BlockSpec((None, LP, LANES), lambda h, i: (h, 0, 0)),
                  pl.BlockSpec((None, F_HD, LP), lambda h, i: (1, h, 0)),
                  pl.BlockSpec((None, 1, F_HD), lambda h, i: (h, 0, 0))],
        out_specs=pl.BlockSpec((TQ, F_HD), lambda h, i: (i, h)),
        out_shape=jax.ShapeDtypeStruct((LP, F_W), bf16),
        scratch_shapes=[pltpu.VMEM((1, TQ), f32),
                        pltpu.VMEM((F_HD + ONES_ROWS, TQ), f32)],
        compiler_params=pltpu.CompilerParams(
            dimension_semantics=("arbitrary", "arbitrary"), vmem_limit_bytes=VMEM_LIMIT),
        name="fox",
    )(proj_t, qbt, proj, kb, proj_t, f_norm)


def _stream_sources(h, head, x):
    if h is None:
        return head, x, 0
    return h, h, 1


def _stream_specs(rest, rest_off, skip):
    rest_block = lambda i: jnp.maximum(i + skip - 1, 0) + rest_off
    if rest.ndim == 3:
        rest_spec = pl.BlockSpec((None, BLK, D_MODEL), lambda i: (0, rest_block(i), 0))
    else:
        rest_spec = pl.BlockSpec((BLK, D_MODEL), lambda i: (rest_block(i), 0))
    return [pl.BlockSpec((BLK, D_MODEL), lambda i: (0, 0)), rest_spec]


def _outproj_kernel(hm_ref, hf_ref, w_ref, first_ref, rest_ref, g_ref, b_ref, of_ref, ob_ref, *, skip):
    half = BLK // 2
    for r in range(2):
        rows = slice(r * half, (r + 1) * half)
        y = jnp.dot(hm_ref[rows, :], w_ref[:M_V_W, :], preferred_element_type=f32)
        y = y + jnp.dot(hf_ref[rows, :], w_ref[M_V_W:, :], preferred_element_type=f32)
        res = rest_ref[rows, :]
        if skip == 0:
            res = jnp.where(pl.program_id(0) == 0, first_ref[rows, :], res)
        out = _layer_norm(DN_ALPHA * res + y, g_ref[...], b_ref[...])
        of_ref[rows, :] = out
        ob_ref[rows, :] = out.astype(bf16)


def _outproj(hm, hf, w, h, head, x, g, b, real_rows_only):
    skip = 1 if real_rows_only else 0
    first, rest, rest_off = _stream_sources(h, head, x)
    rows = LP - skip * BLK
    shifted = lambda i: (i + skip, 0)
    row = lambda i: (i, 0)
    const = lambda i: (0, 0)
    return pl.pallas_call(
        functools.partial(_outproj_kernel, skip=skip),
        grid=(rows // BLK,),
        in_specs=[pl.BlockSpec((BLK, M_V_W), shifted),
                  pl.BlockSpec((BLK, F_W), shifted),
                  pl.BlockSpec((M_V_W + F_W, D_MODEL), const)]
                 + _stream_specs(rest, rest_off, skip)
                 + [pl.BlockSpec((1, D_MODEL), const),
                    pl.BlockSpec((1, D_MODEL), const)],
        out_specs=[pl.BlockSpec((BLK, D_MODEL), row),
                   pl.BlockSpec((BLK, D_MODEL), row)],
        out_shape=[jax.ShapeDtypeStruct((rows, D_MODEL), f32),
                   jax.ShapeDtypeStruct((rows, D_MODEL), bf16)],
        compiler_params=pltpu.CompilerParams(
            dimension_semantics=("arbitrary",), vmem_limit_bytes=VMEM_LIMIT),
        name="outproj",
    )(hm, hf, w, first, rest, g, b)


def _mlp_kernel(xb_ref, wu_ref, wd_ref, h_ref, g_ref, b_ref, of_ref, ob_ref, acc_ref):
    f = pl.program_id(1)

    @pl.when(f == 0)
    def _():
        acc_ref[...] = jnp.zeros_like(acc_ref)

    u = jnp.maximum(jnp.dot(xb_ref[...], wu_ref[...], preferred_element_type=f32), 0.0)
    acc_ref[...] += jnp.dot((u * u).astype(bf16), wd_ref[...], preferred_element_type=f32)

    @pl.when(f == pl.num_programs(1) - 1)
    def _():
        out = _layer_norm(DN_ALPHA * h_ref[...] + acc_ref[...], g_ref[...], b_ref[...])
        of_ref[...] = out
        ob_ref[...] = out.astype(bf16)


def _mlp(xb, wu, wd, h, g, b, batched_out):
    rows = xb.shape[0]
    tm = rows // MLP_ROW_TILES
    assert tm * MLP_ROW_TILES == rows and tm % 8 == 0
    row = lambda i, f: (i, 0)
    const = lambda i, f: (0, 0)
    if batched_out:
        f32_spec = pl.BlockSpec((None, tm, D_MODEL), lambda i, f: (0, i, 0))
        f32_shape = jax.ShapeDtypeStruct((1, rows, D_MODEL), f32)
    else:
        f32_spec = pl.BlockSpec((tm, D_MODEL), row)
        f32_shape = jax.ShapeDtypeStruct((rows, D_MODEL), f32)
    return pl.pallas_call(
        _mlp_kernel,
        grid=(MLP_ROW_TILES, D_FF // TF),
        in_specs=[pl.BlockSpec((tm, D_MODEL), row),
                  pl.BlockSpec((D_MODEL, TF), lambda i, f: (0, f)),
                  pl.BlockSpec((TF, D_MODEL), lambda i, f: (f, 0)),
                  pl.BlockSpec((tm, D_MODEL), row),
                  pl.BlockSpec((1, D_MODEL), const),
                  pl.BlockSpec((1, D_MODEL), const)],
        out_specs=[f32_spec, pl.BlockSpec((tm, D_MODEL), row)],
        out_shape=[f32_shape, jax.ShapeDtypeStruct((rows, D_MODEL), bf16)],
        scratch_shapes=[pltpu.VMEM((tm, D_MODEL), f32)],
        compiler_params=pltpu.CompilerParams(
            dimension_semantics=("arbitrary", "arbitrary"), vmem_limit_bytes=VMEM_LIMIT),
        name="mlp",
    )(xb, wu, wd, h, g, b)


def _cast_kernel(x_ref, o_ref):
    o_ref[...] = x_ref[...].astype(o_ref.dtype)


def _to_bf16(w, l):
    _, rows, cols = w.shape
    tr = CAST_BLOCK_BYTES // (cols * 4)
    return pl.pallas_call(
        _cast_kernel,
        grid=(rows // tr,),
        in_specs=[pl.BlockSpec((None, tr, cols), lambda i: (l, i, 0))],
        out_specs=pl.BlockSpec((tr, cols), lambda i: (i, 0)),
        out_shape=jax.ShapeDtypeStruct((rows, cols), bf16),
        compiler_params=pltpu.CompilerParams(
            dimension_semantics=("arbitrary",), vmem_limit_bytes=VMEM_LIMIT),
        name="cast",
    )(w)


def _prep_w_in_kernel(wt_ref, wg_ref):
    scale = jnp.where(pl.program_id(0) == COL_GROUPS[0], F_HD ** -0.5 * LOG2E, 1.0)
    wg_ref[...] = (wt_ref[0] * scale).astype(bf16)


def _prep_w_in(w_in, l):
    wt = jnp.swapaxes(w_in, 1, 2)
    lead_groups = 2
    gate_rows = 2 * M_HEADS
    sub = 8
    assert GROUP_W % sub == 0 and gate_rows % sub == 0
    first_row = lambda g: sub * (g * (GROUP_W // sub)
                                 + (gate_rows // sub) * (g >= lead_groups).astype(jnp.int32))
    wg = pl.pallas_call(
        _prep_w_in_kernel,
        grid=(W_GROUPS,),
        in_specs=[pl.BlockSpec((pl.Element(1), pl.Element(GROUP_W), pl.Element(D_MODEL)),
                               lambda g: (l, first_row(g), 0))],
        out_specs=pl.BlockSpec((None, GROUP_W, D_MODEL), lambda g: (g, 0, 0)),
        out_shape=jax.ShapeDtypeStruct((W_GROUPS, GROUP_W, D_MODEL), bf16),
        compiler_params=pltpu.CompilerParams(
            dimension_semantics=("arbitrary",), vmem_limit_bytes=VMEM_LIMIT),
        name="prep_w_in",
    )(wt)
    o_gate = lead_groups * GROUP_W
    o_ff = W_GROUPS * GROUP_W + gate_rows
    assert o_ff + F_HEADS == w_in.shape[2]
    gate_t = jnp.concatenate([wt[l, o_gate:o_gate + gate_rows], wt[l, o_ff:o_ff + F_HEADS]], axis=0)
    gate = jnp.zeros((D_MODEL, LANES), f32).at[:, :gate_rows + F_HEADS].set(gate_t.T)
    return wg, gate


def _embed_kernel(first_ref, rest_ref, o_ref):
    o_ref[...] = jnp.where(pl.program_id(0) == 0, first_ref[...], rest_ref[...]).astype(o_ref.dtype)


def _embed_bf16(head, x):
    first, rest, rest_off = _stream_sources(None, head, x)
    return pl.pallas_call(
        _embed_kernel,
        grid=(NBLK,),
        in_specs=_stream_specs(rest, rest_off, 0),
        out_specs=pl.BlockSpec((BLK, D_MODEL), lambda i: (i, 0)),
        out_shape=jax.ShapeDtypeStruct((LP, D_MODEL), bf16),
        compiler_params=pltpu.CompilerParams(dimension_semantics=("arbitrary",)),
        name="embed",
    )(first, rest)


def kernel(x, meta, w_in, conv_w, conv_b, m_i_bias, m_f_bias, m_norm, f_f_bias, f_norm, w_out,
           ln1_g, ln1_b, w_up, w_down, ln2_g, ln2_b):
    assert x.shape == (1, SEQ, D_MODEL)
    head = jnp.concatenate([jnp.zeros((P_PAD, D_MODEL), f32), meta.astype(f32)], axis=0)
    hb = _embed_bf16(head, x)
    h = None
    n_gate = 2 * M_HEADS + F_HEADS
    for l in range(DEPTH):
        last = l == DEPTH - 1
        w_groups, w_gate = _prep_w_in(w_in, l)
        b_gate = jnp.concatenate([m_i_bias[l], m_f_bias[l], f_f_bias[l],
                                  jnp.zeros((LANES - n_gate,), f32)])[None, :]
        G, GT, kb, qbt = _gates(h, head, x, w_gate, b_gate)
        proj = _inproj(hb, w_groups)
        proj_t = _inproj_t(hb, w_groups)
        hm = _mlstm(proj, G, GT, conv_w[l], conv_b[l][None, :], m_norm[l][None, :])
        hf = _fox(proj, proj_t, kb, qbt, f_norm[l].reshape(F_HEADS, 1, F_HD))
        h1, h1b = _outproj(hm, hf, _to_bf16(w_out, l), h, head, x,
                           ln1_g[l][None, :], ln1_b[l][None, :], real_rows_only=last)
        h, hb = _mlp(h1b, _to_bf16(w_up, l), _to_bf16(w_down, l), h1,
                     ln2_g[l][None, :], ln2_b[l][None, :], batched_out=last)
    return h
```

```python
import functools

import jax
import jax.numpy as jnp
from jax import lax
from jax.experimental import pallas as pl
from jax.experimental.pallas import tpu as pltpu

D_MODEL = 2048
SEQ = 16384
DEPTH = 2
N_META = 16
M_HEADS = 4
M_QK = 128
M_V = 256
CONV_W = 4
F_HEADS = 8
F_HD = 128
D_FF = 4 * D_MODEL
M_QK_W = M_HEADS * M_QK
M_V_W = M_HEADS * M_V
F_W = F_HEADS * F_HD
DN_ALPHA = (2 * DEPTH) ** 0.25
LN_EPS = 1e-5
RMS_EPS = 1e-6
NEG = -1e30

LANES = 128
BLK = 256
P_PAD = BLK - N_META
LP = P_PAD + N_META + SEQ
NBLK = LP // BLK
GROUP_W = 1024
W_GROUPS = 6
ROW_GROUPS = (0, 1, 2, 4)
COL_GROUPS = (3, 5)
N_GROUPS = len(ROW_GROUPS)
N_GROUPS_T = len(COL_GROUPS)
TQ = 1280
SUB = TQ // BLK
N_BIAS = 3
ONES_ROWS = 16
UNROLL = 2
LOOKAHEAD = 5
LOG2E = 1.4426950408889634
TM_PROJ = 1040
MLP_ROW_TILES = 32
TF = 1024
C_AUG = M_V + LANES
VMEM_LIMIT = 56 * 1024 * 1024
CAST_BLOCK_BYTES = 8 * 1024 * 1024

f32 = jnp.float32
bf16 = jnp.bfloat16


def _layer_norm(z, g, b):
    mu = jnp.mean(z, axis=-1, keepdims=True)
    zc = z - mu
    var = jnp.mean(zc * zc, axis=-1, keepdims=True)
    return zc * lax.rsqrt(var + LN_EPS) * g + b


def _log_sigmoid(x):
    return jnp.minimum(x, 0.0) - jnp.log(1.0 + jnp.exp(-jnp.abs(x)))


def _bf16_piece(x):
    return x.astype(bf16).astype(f32)


def _gates_kernel(first_ref, rest_ref, w_ref, b_ref, g_ref, gt_ref, kb_ref, qbt_ref,
                  carry_ref, wsplit_ref):
    i = pl.program_id(0)

    @pl.when(i == 0)
    def _():
        carry_ref[...] = jnp.zeros_like(carry_ref)
        w = w_ref[...]
        w_hi = _bf16_piece(w)
        wsplit_ref[:, :LANES] = w_hi.astype(bf16)
        wsplit_ref[:, LANES:] = (w - w_hi).astype(bf16)

    x = jnp.where(i == 0, first_ref[...], rest_ref[...])
    x_hi = _bf16_piece(x)
    both = jnp.dot(x_hi.astype(bf16), wsplit_ref[...], preferred_element_type=f32)
    low = jnp.dot((x - x_hi).astype(bf16), wsplit_ref[:, :LANES], preferred_element_type=f32)
    pre = both[:, :LANES] + (both[:, LANES:] + low) + b_ref[...]
    row = i * BLK + lax.broadcasted_iota(jnp.int32, (BLK, LANES), 0)
    col = lax.broadcasted_iota(jnp.int32, (BLK, LANES), 1)
    valid = row >= P_PAD
    is_f = (col >= M_HEADS) & (col < 2 * M_HEADS + F_HEADS)
    lf = jnp.where(valid & is_f, _log_sigmoid(pre), 0.0)
    r2 = lax.broadcasted_iota(jnp.int32, (BLK, BLK), 0)
    c2 = lax.broadcasted_iota(jnp.int32, (BLK, BLK), 1)
    tri = jnp.where(c2 <= r2, 1.0, 0.0).astype(f32)
    cum = jnp.dot(tri, lf, preferred_element_type=f32,
                  precision=lax.Precision.HIGHEST) + carry_ref[...]
    carry_ref[...] = cum[BLK - 1:BLK, :]
    out = jnp.where(col < M_HEADS, jnp.where(valid, pre, NEG), cum)
    g_ref[...] = out
    gt_ref[...] = out.T

    valid_col = i * BLK + lax.broadcasted_iota(jnp.int32, (BLK, 1), 0) >= P_PAD
    for h in range(F_HEADS):
        c = cum[:, 2 * M_HEADS + h:2 * M_HEADS + h + 1] * LOG2E
        pieces = []
        rest = c
        for _ in range(N_BIAS):
            piece = _bf16_piece(rest)
            pieces.append(piece)
            rest = rest - piece
        tile_k = jnp.where(col < N_BIAS, 1.0, 0.0)
        tile_q = jnp.where((col >= N_BIAS) & (col < 2 * N_BIAS), 1.0, 0.0)
        for n, piece in enumerate(pieces):
            key_piece = jnp.where(valid_col, -piece, NEG) if n == 0 else -piece
            tile_k = jnp.where(col == N_BIAS + n, key_piece, tile_k)
            tile_q = jnp.where(col == n, piece, tile_q)
        kb_ref[h] = tile_k.astype(bf16)
        qbt_ref[h] = tile_q.T.astype(bf16)


def _gates(h, head, x, wg, bg):
    first, rest, rest_off = _stream_sources(h, head, x)
    return pl.pallas_call(
        _gates_kernel,
        grid=(NBLK,),
        in_specs=_stream_specs(rest, rest_off, 0)
                 + [pl.BlockSpec((D_MODEL, LANES), lambda i: (0, 0)),
                    pl.BlockSpec((1, LANES), lambda i: (0, 0))],
        out_specs=[pl.BlockSpec((BLK, LANES), lambda i: (i, 0)),
                   pl.BlockSpec((LANES, BLK), lambda i: (0, i)),
                   pl.BlockSpec((F_HEADS, BLK, LANES), lambda i: (0, i, 0)),
                   pl.BlockSpec((F_HEADS, LANES, BLK), lambda i: (0, 0, i))],
        out_shape=[jax.ShapeDtypeStruct((LP, LANES), f32),
                   jax.ShapeDtypeStruct((LANES, LP), f32),
                   jax.ShapeDtypeStruct((F_HEADS, LP, LANES), bf16),
                   jax.ShapeDtypeStruct((F_HEADS, LANES, LP), bf16)],
        scratch_shapes=[pltpu.VMEM((1, LANES), f32),
                        pltpu.VMEM((D_MODEL, 2 * LANES), bf16)],
        compiler_params=pltpu.CompilerParams(dimension_semantics=("arbitrary",)),
        name="gates",
    )(first, rest, wg, bg)


def _inproj_kernel(x_ref, w_ref, o_ref):
    o_ref[...] = lax.dot_general(x_ref[...], w_ref[...], (((1,), (1,)), ((), ())),
                                 preferred_element_type=f32).astype(o_ref.dtype)


def _inproj(xb, wg):
    assert ROW_GROUPS == (0, 1, 2, 4)
    return pl.pallas_call(
        _inproj_kernel,
        grid=(N_GROUPS, LP // TM_PROJ),
        in_specs=[pl.BlockSpec((TM_PROJ, D_MODEL), lambda j, i: (i, 0)),
                  pl.BlockSpec((None, GROUP_W, D_MODEL),
                               lambda j, i: (j + (j >= 3).astype(jnp.int32), 0, 0))],
        out_specs=pl.BlockSpec((None, TM_PROJ, GROUP_W), lambda j, i: (j, i, 0)),
        out_shape=jax.ShapeDtypeStruct((N_GROUPS, LP, GROUP_W), bf16),
        compiler_params=pltpu.CompilerParams(
            dimension_semantics=("arbitrary", "arbitrary"), vmem_limit_bytes=VMEM_LIMIT),
        name="inproj",
    )(xb, wg)


def _inproj_t_kernel(x_ref, wt_ref, o_ref):
    o_ref[...] = lax.dot_general(wt_ref[...], x_ref[...], (((1,), (1,)), ((), ())),
                                 preferred_element_type=f32).astype(o_ref.dtype)


def _inproj_t(xb, wg):
    assert COL_GROUPS == (3, 5)
    return pl.pallas_call(
        _inproj_t_kernel,
        grid=(N_GROUPS_T, LP // TQ),
        in_specs=[pl.BlockSpec((TQ, D_MODEL), lambda j, i: (i, 0)),
                  pl.BlockSpec((None, GROUP_W, D_MODEL), lambda j, i: (3 + 2 * j, 0, 0))],
        out_specs=pl.BlockSpec((None, GROUP_W, TQ), lambda j, i: (j, 0, i)),
        out_shape=jax.ShapeDtypeStruct((N_GROUPS_T, GROUP_W, LP), bf16),
        compiler_params=pltpu.CompilerParams(
            dimension_semantics=("arbitrary", "arbitrary"), vmem_limit_bytes=VMEM_LIMIT),
        name="inproj_t",
    )(xb, wg)


def _mlstm_kernel(qk_ref, v_ref, og_ref, g_ref, gt_ref, cw_ref, cb_ref, nrm_ref, out_ref,
                  c_scr, m_scr, cf_scr, conv_scr):
    i = pl.program_id(0)
    T = BLK

    @pl.when(i == 0)
    def _():
        c_scr[...] = jnp.zeros_like(c_scr)
        m_scr[...] = jnp.zeros_like(m_scr)
        cf_scr[...] = jnp.zeros_like(cf_scr)
        conv_scr[...] = jnp.zeros_like(conv_scr)

    row = i * T + lax.broadcasted_iota(jnp.int32, (T, 1), 0)
    x = jnp.where(row >= P_PAD, qk_ref[...].astype(f32), 0.0)
    xs = jnp.concatenate([conv_scr[...], x], axis=0)
    cw = cw_ref[...]
    conv = cb_ref[...]
    for k in range(CONV_W):
        off = 8 - (CONV_W - 1) + k
        conv = conv + cw[k:k + 1, :] * xs[off:off + T, :]
    conv_scr[...] = x[T - 8:T, :]
    act = conv * jax.nn.sigmoid(conv)

    G = g_ref[...]
    GT = gt_ref[...]
    cfs_all = cf_scr[...]
    cf_scr[...] = G[T - 1:T, :]
    r2 = lax.broadcasted_iota(jnp.int32, (T, T), 0)
    c2 = lax.broadcasted_iota(jnp.int32, (T, T), 1)
    causal = c2 <= r2
    ones_blk = jnp.where(lax.broadcasted_iota(jnp.int32, (T, LANES), 1) == 0, 1.0, 0.0).astype(bf16)

    for h in range(M_HEADS):
        qh = (act[:, h * M_QK:(h + 1) * M_QK] * (M_QK ** -0.5)).astype(bf16)
        kf = act[:, M_QK_W + h * M_QK:M_QK_W + (h + 1) * M_QK]
        kh = kf.astype(bf16)
        v_aug = jnp.concatenate([v_ref[:, h * M_V:(h + 1) * M_V], ones_blk], axis=1)
        cfs = cfs_all[:, M_HEADS + h:M_HEADS + h + 1]
        b_col = G[:, M_HEADS + h:M_HEADS + h + 1] - cfs
        b_row = GT[M_HEADS + h:M_HEADS + h + 1, :] - cfs
        li_col = G[:, h:h + 1]
        li_row = GT[h:h + 1, :]
        g_tot = b_col[T - 1:T, :]
        m_prev = m_scr[h][:, 0:1]
        c_prev = c_scr[h]

        d = jnp.where(causal, b_col - b_row + li_row, NEG)
        inter_log = b_col + m_prev
        m_t = jnp.maximum(inter_log, jnp.max(d, axis=-1, keepdims=True))
        s = lax.dot_general(qh, kh, (((1,), (1,)), ((), ())), preferred_element_type=f32)
        w = (jnp.exp(d - m_t) * s).astype(bf16)
        s_inter = jnp.exp(inter_log - m_t)
        num = jnp.dot(w, v_aug, preferred_element_type=f32) + s_inter * jnp.dot(
            qh, c_prev.astype(bf16), preferred_element_type=f32)
        nq = num[:, M_V:M_V + 1]
        hv = num[:, :M_V] / jnp.maximum(jnp.abs(nq), jnp.exp(-m_t))
        hv = hv * lax.rsqrt(jnp.mean(hv * hv, axis=-1, keepdims=True) + RMS_EPS)
        gate = jax.nn.sigmoid(og_ref[:, h * M_V:(h + 1) * M_V].astype(f32))
        out_ref[:, h * M_V:(h + 1) * M_V] = (
            hv * nrm_ref[:, h * M_V:(h + 1) * M_V] * gate).astype(out_ref.dtype)

        a_col = g_tot - b_col + li_col
        m_loc = jnp.max(a_col, axis=0, keepdims=True)
        kw_t = (kf * jnp.exp(a_col - m_loc)).T.astype(bf16)
        c_loc = jnp.dot(kw_t, v_aug, preferred_element_type=f32)
        m_new = jnp.maximum(g_tot + m_prev, m_loc)
        c_scr[h] = jnp.exp(g_tot + m_prev - m_new) * c_prev + jnp.exp(m_loc - m_new) * c_loc
        m_scr[h] = jnp.broadcast_to(m_new, (1, LANES))


def _mlstm(proj, G, GT, conv_w, conv_b, m_norm):
    return pl.pallas_call(
        _mlstm_kernel,
        grid=(NBLK,),
        in_specs=[pl.BlockSpec((None, BLK, GROUP_W), lambda i: (0, i, 0)),
                  pl.BlockSpec((None, BLK, GROUP_W), lambda i: (1, i, 0)),
                  pl.BlockSpec((None, BLK, GROUP_W), lambda i: (2, i, 0)),
                  pl.BlockSpec((BLK, LANES), lambda i: (i, 0)),
                  pl.BlockSpec((8, BLK), lambda i: (0, i)),
                  pl.BlockSpec((CONV_W, 2 * M_QK_W), lambda i: (0, 0)),
                  pl.BlockSpec((1, 2 * M_QK_W), lambda i: (0, 0)),
                  pl.BlockSpec((1, M_V_W), lambda i: (0, 0))],
        out_specs=pl.BlockSpec((BLK, M_V_W), lambda i: (i, 0)),
        out_shape=jax.ShapeDtypeStruct((LP, M_V_W), bf16),
        scratch_shapes=[pltpu.VMEM((M_HEADS, M_QK, C_AUG), f32),
                        pltpu.VMEM((M_HEADS, 1, LANES), f32),
                        pltpu.VMEM((1, LANES), f32),
                        pltpu.VMEM((8, 2 * M_QK_W), f32)],
        compiler_params=pltpu.CompilerParams(
            dimension_semantics=("arbitrary",), vmem_limit_bytes=VMEM_LIMIT),
        name="mlstm",
    )(proj, proj, proj, G, GT, conv_w, conv_b, m_norm)


def _fox_kernel(qt_ref, qbt_ref, k_ref, kb_ref, vt_ref, nrm_ref, o_ref, m_scr, acc_scr):
    qi = pl.program_id(1)
    q_aug = jnp.concatenate([qt_ref[...], qbt_ref[...]], axis=0)
    m_scr[...] = jnp.full_like(m_scr, NEG)
    acc_scr[...] = jnp.zeros_like(acc_scr)

    ones_rows = jnp.ones((ONES_ROWS, BLK), bf16)

    def scores(c, start, masked):
        cols = slice(c * BLK, (c + 1) * BLK)
        k_aug = jnp.concatenate([k_ref[pl.ds(start, BLK), :], kb_ref[pl.ds(start, BLK), :]], axis=1)
        s = jnp.dot(k_aug, q_aug[:, cols], preferred_element_type=f32)
        if masked:
            kpos = start + lax.broadcasted_iota(jnp.int32, (BLK, 1), 0)
            qpos = qi * TQ + c * BLK + lax.broadcasted_iota(jnp.int32, (1, BLK), 1)
            s = jnp.where(kpos <= qpos, s, NEG)
        m_prev = m_scr[:, cols]
        m_new = jnp.maximum(m_prev, jnp.max(s, axis=0, keepdims=True))
        m_scr[:, cols] = m_new
        return s, m_prev, m_new

    def accumulate(c, start, s, m_prev, m_new):
        cols = slice(c * BLK, (c + 1) * BLK)
        v_aug = jnp.concatenate([vt_ref[:, pl.ds(start, BLK)], ones_rows], axis=0)
        p = jnp.exp2(s - m_new).astype(bf16)
        alpha = jnp.exp2(m_prev - m_new)
        acc_scr[:, cols] = alpha * acc_scr[:, cols] + jnp.dot(v_aug, p, preferred_element_type=f32)

    def run(units):
        pending = []
        for c, start, masked in units:
            pending.append((c, start) + scores(c, start, masked))
            if len(pending) > LOOKAHEAD:
                accumulate(*pending.pop(0))
        for unit in pending:
            accumulate(*unit)

    def full_blocks(first_key, n_blocks):
        run([(c, pl.multiple_of(first_key + d * BLK, BLK), False)
             for d in range(n_blocks) for c in range(SUB)])

    def body(j, carry):
        full_blocks(pl.multiple_of(j * (UNROLL * TQ), TQ), UNROLL * SUB)
        return carry

    lax.fori_loop(0, qi // UNROLL, body, 0)
    for r in range(1, UNROLL):
        @pl.when(qi % UNROLL >= r)
        def _():
            full_blocks(pl.multiple_of((qi // UNROLL * UNROLL + r - 1) * TQ, TQ), SUB)

    tail = pl.multiple_of(qi * TQ, TQ)
    run([(c, pl.multiple_of(tail + d * BLK, BLK), c == d) for d in range(SUB) for c in range(d, SUB)])

    acc = acc_scr[...]
    ot = acc[:F_HD, :] / acc[F_HD:F_HD + 1, :]
    ot = ot * lax.rsqrt(jnp.mean(ot * ot, axis=0, keepdims=True) + RMS_EPS)
    o_ref[...] = (ot.T * nrm_ref[...]).astype(o_ref.dtype)


def _fox(proj, proj_t, kb, qbt, f_norm):
    return pl.pallas_call(
        _fox_kernel,
        grid=(F_HEADS, LP // TQ),
        in_specs=[pl.BlockSpec((None, F_HD, TQ), lambda h, i: (0, h, i)),
                  pl.BlockSpec((None, LANES, TQ), lambda h, i: (h, 0, i)),
                  pl.BlockSpec((None, LP, F_HD), lambda h, i: (3, 0, h)),
                  pl.BlockSpec((None, LP, LANES), lambda h, i: (h, 0, 0)),
                  pl.BlockSpec((None, F_HD, LP), lambda h, i: (1, h, 0)),
                  pl.BlockSpec((None, 1, F_HD), lambda h, i: (h, 0, 0))],
        out_specs=pl.BlockSpec((TQ, F_HD), lambda h, i: (i, h)),
        out_shape=jax.ShapeDtypeStruct((LP, F_W), bf16),
        scratch_shapes=[pltpu.VMEM((1, TQ), f32),
                        pltpu.VMEM((F_HD + ONES_ROWS, TQ), f32)],
        compiler_params=pltpu.CompilerParams(
            dimension_semantics=("arbitrary", "arbitrary"), vmem_limit_bytes=VMEM_LIMIT),
        name="fox",
    )(proj_t, qbt, proj, kb, proj_t, f_norm)


def _stream_sources(h, head, x):
    if h is None:
        return head, x, 0
    return h, h, 1


def _stream_specs(rest, rest_off, skip):
    rest_block = lambda i: jnp.maximum(i + skip - 1, 0) + rest_off
    if rest.ndim == 3:
        rest_spec = pl.BlockSpec((None, BLK, D_MODEL), lambda i: (0, rest_block(i), 0))
    else:
        rest_spec = pl.BlockSpec((BLK, D_MODEL), lambda i: (rest_block(i), 0))
    return [pl.BlockSpec((BLK, D_MODEL), lambda i: (0, 0)), rest_spec]


def _outproj_kernel(hm_ref, hf_ref, w_ref, first_ref, rest_ref, g_ref, b_ref, of_ref, ob_ref, *, skip):
    half = BLK // 2
    for r in range(2):
        rows = slice(r * half, (r + 1) * half)
        y = jnp.dot(hm_ref[rows, :], w_ref[:M_V_W, :], preferred_element_type=f32)
        y = y + jnp.dot(hf_ref[rows, :], w_ref[M_V_W:, :], preferred_element_type=f32)
        res = rest_ref[rows, :]
        if skip == 0:
            res = jnp.where(pl.program_id(0) == 0, first_ref[rows, :], res)
        out = _layer_norm(DN_ALPHA * res + y, g_ref[...], b_ref[...])
        of_ref[rows, :] = out
        ob_ref[rows, :] = out.astype(bf16)


def _outproj(hm, hf, w, h, head, x, g, b, real_rows_only):
    skip = 1 if real_rows_only else 0
    first, rest, rest_off = _stream_sources(h, head, x)
    rows = LP - skip * BLK
    shifted = lambda i: (i + skip, 0)
    row = lambda i: (i, 0)
    const = lambda i: (0, 0)
    return pl.pallas_call(
        functools.partial(_outproj_kernel, skip=skip),
        grid=(rows // BLK,),
        in_specs=[pl.BlockSpec((BLK, M_V_W), shifted),
                  pl.BlockSpec((BLK, F_W), shifted),
                  pl.BlockSpec((M_V_W + F_W, D_MODEL), const)]
                 + _stream_specs(rest, rest_off, skip)
                 + [pl.BlockSpec((1, D_MODEL), const),
                    pl.BlockSpec((1, D_MODEL), const)],
        out_specs=[pl.BlockSpec((BLK, D_MODEL), row),
                   pl.BlockSpec((BLK, D_MODEL), row)],
        out_shape=[jax.ShapeDtypeStruct((rows, D_MODEL), f32),
                   jax.ShapeDtypeStruct((rows, D_MODEL), bf16)],
        compiler_params=pltpu.CompilerParams(
            dimension_semantics=("arbitrary",), vmem_limit_bytes=VMEM_LIMIT),
        name="outproj",
    )(hm, hf, w, first, rest, g, b)


def _mlp_kernel(xb_ref, wu_ref, wd_ref, h_ref, g_ref, b_ref, of_ref, ob_ref, acc_ref):
    f = pl.program_id(1)

    @pl.when(f == 0)
    def _():
        acc_ref[...] = jnp.zeros_like(acc_ref)

    u = jnp.maximum(jnp.dot(xb_ref[...], wu_ref[...], preferred_element_type=f32), 0.0)
    acc_ref[...] += jnp.dot((u * u).astype(bf16), wd_ref[...], preferred_element_type=f32)

    @pl.when(f == pl.num_programs(1) - 1)
    def _():
        out = _layer_norm(DN_ALPHA * h_ref[...] + acc_ref[...], g_ref[...], b_ref[...])
        of_ref[...] = out
        ob_ref[...] = out.astype(bf16)


def _mlp(xb, wu, wd, h, g, b, batched_out):
    rows = xb.shape[0]
    tm = rows // MLP_ROW_TILES
    assert tm * MLP_ROW_TILES == rows and tm % 8 == 0
    row = lambda i, f: (i, 0)
    const = lambda i, f: (0, 0)
    if batched_out:
        f32_spec = pl.BlockSpec((None, tm, D_MODEL), lambda i, f: (0, i, 0))
        f32_shape = jax.ShapeDtypeStruct((1, rows, D_MODEL), f32)
    else:
        f32_spec = pl.BlockSpec((tm, D_MODEL), row)
        f32_shape = jax.ShapeDtypeStruct((rows, D_MODEL), f32)
    return pl.pallas_call(
        _mlp_kernel,
        grid=(MLP_ROW_TILES, D_FF // TF),
        in_specs=[pl.BlockSpec((tm, D_MODEL), row),
                  pl.BlockSpec((D_MODEL, TF), lambda i, f: (0, f)),
                  pl.BlockSpec((TF, D_MODEL), lambda i, f: (f, 0)),
                  pl.BlockSpec((tm, D_MODEL), row),
                  pl.BlockSpec((1, D_MODEL), const),
                  pl.BlockSpec((1, D_MODEL), const)],
        out_specs=[f32_spec, pl.BlockSpec((tm, D_MODEL), row)],
        out_shape=[f32_shape, jax.ShapeDtypeStruct((rows, D_MODEL), bf16)],
        scratch_shapes=[pltpu.VMEM((tm, D_MODEL), f32)],
        compiler_params=pltpu.CompilerParams(
            dimension_semantics=("arbitrary", "arbitrary"), vmem_limit_bytes=VMEM_LIMIT),
        name="mlp",
    )(xb, wu, wd, h, g, b)


def _cast_kernel(x_ref, o_ref):
    o_ref[...] = x_ref[...].astype(o_ref.dtype)


def _to_bf16(w, l):
    _, rows, cols = w.shape
    tr = CAST_BLOCK_BYTES // (cols * 4)
    return pl.pallas_call(
        _cast_kernel,
        grid=(rows // tr,),
        in_specs=[pl.BlockSpec((None, tr, cols), lambda i: (l, i, 0))],
        out_specs=pl.BlockSpec((tr, cols), lambda i: (i, 0)),
        out_shape=jax.ShapeDtypeStruct((rows, cols), bf16),
        compiler_params=pltpu.CompilerParams(
            dimension_semantics=("arbitrary",), vmem_limit_bytes=VMEM_LIMIT),
        name="cast",
    )(w)


def _prep_w_in_kernel(wt_ref, wg_ref):
    scale = jnp.where(pl.program_id(0) == COL_GROUPS[0], F_HD ** -0.5 * LOG2E, 1.0)
    wg_ref[...] = (wt_ref[0] * scale).astype(bf16)


def _prep_w_in(w_in, l):
    wt = jnp.swapaxes(w_in, 1, 2)
    lead_groups = 2
    gate_rows = 2 * M_HEADS
    sub = 8
    assert GROUP_W % sub == 0 and gate_rows % sub == 0
    first_row = lambda g: sub * (g * (GROUP_W // sub)
                                 + (gate_rows // sub) * (g >= lead_groups).astype(jnp.int32))
    wg = pl.pallas_call(
        _prep_w_in_kernel,
        grid=(W_GROUPS,),
        in_specs=[pl.BlockSpec((pl.Element(1), pl.Element(GROUP_W), pl.Element(D_MODEL)),
                               lambda g: (l, first_row(g), 0))],
        out_specs=pl.BlockSpec((None, GROUP_W, D_MODEL), lambda g: (g, 0, 0)),
        out_shape=jax.ShapeDtypeStruct((W_GROUPS, GROUP_W, D_MODEL), bf16),
        compiler_params=pltpu.CompilerParams(
            dimension_semantics=("arbitrary",), vmem_limit_bytes=VMEM_LIMIT),
        name="prep_w_in",
    )(wt)
    o_gate = lead_groups * GROUP_W
    o_ff = W_GROUPS * GROUP_W + gate_rows
    assert o_ff + F_HEADS == w_in.shape[2]
    gate_t = jnp.concatenate([wt[l, o_gate:o_gate + gate_rows], wt[l, o_ff:o_ff + F_HEADS]], axis=0)
    gate = jnp.zeros((D_MODEL, LANES), f32).at[:, :gate_rows + F_HEADS].set(gate_t.T)
    return wg, gate


def _embed_kernel(first_ref, rest_ref, o_ref):
    o_ref[...] = jnp.where(pl.program_id(0) == 0, first_ref[...], rest_ref[...]).astype(o_ref.dtype)


def _embed_bf16(head, x):
    first, rest, rest_off = _stream_sources(None, head, x)
    return pl.pallas_call(
        _embed_kernel,
        grid=(NBLK,),
        in_specs=_stream_specs(rest, rest_off, 0),
        out_specs=pl.BlockSpec((BLK, D_MODEL), lambda i: (i, 0)),
        out_shape=jax.ShapeDtypeStruct((LP, D_MODEL), bf16),
        compiler_params=pltpu.CompilerParams(dimension_semantics=("arbitrary",)),
        name="embed",
    )(first, rest)


def kernel(x, meta, w_in, conv_w, conv_b, m_i_bias, m_f_bias, m_norm, f_f_bias, f_norm, w_out,
           ln1_g, ln1_b, w_up, w_down, ln2_g, ln2_b):
    assert x.shape == (1, SEQ, D_MODEL)
    head = jnp.concatenate([jnp.zeros((P_PAD, D_MODEL), f32), meta.astype(f32)], axis=0)
    hb = _embed_bf16(head, x)
    h = None
    n_gate = 2 * M_HEADS + F_HEADS
    for l in range(DEPTH):
        last = l == DEPTH - 1
        w_groups, w_gate = _prep_w_in(w_in, l)
        b_gate = jnp.concatenate([m_i_bias[l], m_f_bias[l], f_f_bias[l],
                                  jnp.zeros((LANES - n_gate,), f32)])[None, :]
        G, GT, kb, qbt = _gates(h, head, x, w_gate, b_gate)
        proj = _inproj(hb, w_groups)
        proj_t = _inproj_t(hb, w_groups)
        hm = _mlstm(proj, G, GT, conv_w[l], conv_b[l][None, :], m_norm[l][None, :])
        hf = _fox(proj, proj_t, kb, qbt, f_norm[l].reshape(F_HEADS, 1, F_HD))
        h1, h1b = _outproj(hm, hf, _to_bf16(w_out, l), h, head, x,
                           ln1_g[l][None, :], ln1_b[l][None, :], real_rows_only=last)
        h, hb = _mlp(h1b, _to_bf16(w_up, l), _to_bf16(w_down, l), h1,
                     ln2_g[l][None, :], ln2_b[l][None, :], batched_out=last)
    return h
```
